```python
import jax, jax.numpy as jnp
from jax import lax
import numpy as np

D_MODEL = 1024
BATCH = 8
SEQ = 8192
DEPTH = 4
DEC_BATCH = 1
DEC_SEQ = 16384
PAST_LEN = 128

MIX_WIDTH = D_MODEL
N_HEADS = 8
HEAD_DIM = 64
ATTN_WIDTH = N_HEADS * HEAD_DIM
CONV_CH = MIX_WIDTH - ATTN_WIDTH
CONV_WIDTH = 31
DILATED_BRANCHES = ((128, 1), (512, 4), (2048, 16))
Q_BLOCK = 64
N_EXPERTS = 32
TOP_K = 4
D_FF = D_MODEL
SWIGLU_ALPHA = 1.702
SWIGLU_LIMIT = 7.0
TOKEN_CHUNK = 128
NORM_EPS = 1e-6
NEG_BIG = -1e30
IN_WIDTH = 3 * ATTN_WIDTH + 2 * CONV_CH

kernel_name = 'hybrid_dilated_attn_conformer_moe_encoder'


def rms_norm(x, g):
    xf = x.astype(jnp.float32)
    y = xf * lax.rsqrt(jnp.mean(xf * xf, axis=-1, keepdims=True) + NORM_EPS)
    return (y * g.astype(jnp.float32)).astype(x.dtype)


def layer_norm(x, g, b):
    xf = x.astype(jnp.float32)
    mu = jnp.mean(xf, axis=-1, keepdims=True)
    xc = xf - mu
    y = xc * lax.rsqrt(jnp.mean(xc * xc, axis=-1, keepdims=True) + NORM_EPS)
    return (y * g.astype(jnp.float32) + b.astype(jnp.float32)).astype(x.dtype)


def alibi_slopes():
    return 2.0 ** (-8.0 * jnp.arange(1, N_HEADS + 1, dtype=jnp.float32) / N_HEADS)


def dilated_branch(q, k, v, slopes, half, dilation):
    B, S, H, Dh = q.shape
    L = S // dilation
    N = B * dilation
    nb = -(-L // Q_BLOCK)
    Lp = nb * Q_BLOCK

    def split(t):
        return t.reshape(B, L, dilation, H, Dh).transpose(0, 2, 1, 3, 4).reshape(N, L, H, Dh)

    def merge(t):
        rest = t.shape[2:]
        perm = (0, 2, 1) + tuple(range(3, 3 + len(rest)))
        return t.reshape((B, dilation, L) + rest).transpose(perm).reshape((B, S) + rest)

    def windows(t):
        tb = jnp.pad(t, ((0, 0), (Q_BLOCK, Lp - L + Q_BLOCK), (0, 0), (0, 0)))
        tb = tb.reshape(N, nb + 2, Q_BLOCK, H, Dh)
        return jnp.concatenate([tb[:, :-2], tb[:, 1:-1], tb[:, 2:]], axis=2)

    qb = jnp.pad(split(q), ((0, 0), (0, Lp - L), (0, 0), (0, 0))).reshape(N, nb, Q_BLOCK, H, Dh)
    kw = windows(split(k))
    vw = windows(split(v))
    s = jnp.einsum('nbqhd,nbkhd->nbhqk', qb, kw) * (HEAD_DIM ** -0.5)

    koff = jnp.arange(3 * Q_BLOCK) - Q_BLOCK
    rel = koff[None, :] - jnp.arange(Q_BLOCK)[:, None]
    kpos = jnp.arange(nb)[:, None, None] * Q_BLOCK + koff[None, None, :]
    valid = (jnp.abs(rel) <= half)[None] & (kpos >= 0) & (kpos < L)
    dist = (dilation * jnp.abs(rel)).astype(jnp.float32)
    s = s - slopes[:, None, None] * dist[None]
    s = jnp.where(valid[None, :, None], s, NEG_BIG)

    m = jnp.max(s, axis=-1, keepdims=True)
    p = jnp.exp(s - m)
    den = jnp.sum(p, axis=-1, keepdims=True)
    o = jnp.einsum('nbhqk,nbkhd->nbqhd', p / den, vw)
    lse = (m + jnp.log(den))[..., 0]
    o = o.reshape(N, Lp, H, Dh)[:, :L]
    lse = lse.transpose(0, 1, 3, 2).reshape(N, Lp, H)[:, :L]
    return merge(o), merge(lse)


def dilated_attention(q, k, v):
    slopes = alibi_slopes()
    outs, lses = [], []
    for window, dilation in DILATED_BRANCHES:
        o, l = dilated_branch(q, k, v, slopes, window // (2 * dilation), dilation)
        outs.append(o)
        lses.append(l)
    w = jax.nn.softmax(jnp.stack(lses, 0), axis=0)
    return jnp.einsum('rbsh,rbshd->bshd', w, jnp.stack(outs, 0))


def hybrid_mixer(h, w_in, conv_w, conv_b, conv_ln_g, conv_ln_b, g_attn_out, g_conv_out, w_out):
    B, S, _ = h.shape
    proj = h @ w_in
    q, k, v, cv, cg = jnp.split(
        proj, [ATTN_WIDTH, 2 * ATTN_WIDTH, 3 * ATTN_WIDTH, 3 * ATTN_WIDTH + CONV_CH], axis=-1)

    def heads(t):
        return t.reshape(B, S, N_HEADS, HEAD_DIM).astype(jnp.float32)

    attn = dilated_attention(heads(q), heads(k), heads(v)).reshape(B, S, ATTN_WIDTH).astype(h.dtype)

    u = cv * jax.nn.sigmoid(cg)
    u = lax.conv_general_dilated(
        u, conv_w[:, None, :], window_strides=(1,),
        padding=[(CONV_WIDTH // 2, CONV_WIDTH // 2)],
        dimension_numbers=('NWC', 'WIO', 'NWC'),
        feature_group_count=CONV_CH) + conv_b
    u = jax.nn.silu(layer_norm(u, conv_ln_g, conv_ln_b))

    merged = jnp.concatenate([rms_norm(attn, g_attn_out), rms_norm(u, g_conv_out)], axis=-1)
    return merged @ w_out


def routed_ffn(h, w_router, b_router, w_gate_up, b_gate_up, w_down, b_down):
    B, S, D = h.shape
    chunks = h.reshape(-1, TOKEN_CHUNK, D)

    def one_chunk(xc):
        logits = (xc @ w_router + b_router).astype(jnp.float32)
        top_v, top_i = lax.top_k(logits, TOP_K)
        top_w = jax.nn.softmax(top_v, axis=-1)
        gates = jnp.sum(jax.nn.one_hot(top_i, N_EXPERTS, dtype=jnp.float32) * top_w[..., None], axis=1)
        gates = gates.astype(xc.dtype)
        gu = jnp.einsum('td,edf->tef', xc, w_gate_up) + b_gate_up
        gate = jnp.minimum(gu[..., :D_FF], SWIGLU_LIMIT)
        up = jnp.clip(gu[..., D_FF:], -SWIGLU_LIMIT, SWIGLU_LIMIT)
        act = (up + 1.0) * gate * jax.nn.sigmoid(SWIGLU_ALPHA * gate)
        act = act * gates[..., None]
        return jnp.einsum('tef,efd->td', act, w_down) + gates @ b_down

    return lax.map(one_chunk, chunks).reshape(B, S, D)


def encoder_trunk(x, c, w_ada, b_ada, g_norm1, w_in, conv_w, conv_b, conv_ln_g, conv_ln_b,
                  g_attn_out, g_conv_out, w_out, g_norm2, w_router, b_router,
                  w_gate_up, b_gate_up, w_down, b_down, g_final):
    for l in range(DEPTH):
        mod = jax.nn.silu(c) @ w_ada[l] + b_ada[l]
        sh1, sc1, gt1, sh2, sc2, gt2 = [t[:, None, :] for t in jnp.split(mod, 6, axis=-1)]
        h = rms_norm(x, g_norm1[l]) * (1.0 + sc1) + sh1
        x = x + gt1 * hybrid_mixer(h, w_in[l], conv_w[l], conv_b[l], conv_ln_g[l], conv_ln_b[l],
                                   g_attn_out[l], g_conv_out[l], w_out[l])
        h = rms_norm(x, g_norm2[l]) * (1.0 + sc2) + sh2
        x = x + gt2 * routed_ffn(h, w_router[l], b_router[l], w_gate_up[l], b_gate_up[l],
                                 w_down[l], b_down[l])
    return rms_norm(x, g_final)


def setup_inputs(seed: int = 0) -> dict:
    key = jax.random.key(seed)
    ks = jax.random.split(key, 24)
    D = D_MODEL

    def nrm(k, shape, scale):
        return jax.random.normal(k, shape, jnp.float32) * scale

    return {
        'x_prompt': nrm(ks[0], (BATCH, SEQ, D), 1.0),
        'x_sample': nrm(ks[1], (DEC_BATCH, DEC_SEQ, D), 1.0),
        'c_prompt': nrm(ks[2], (BATCH, D), 1.0),
        'c_sample': nrm(ks[3], (DEC_BATCH, D), 1.0),
        'w_ada': nrm(ks[4], (DEPTH, D, 6 * D), 0.5 * D ** -0.5),
        'b_ada': nrm(ks[5], (DEPTH, 6 * D), 0.02),
        'g_norm1': 1.0 + nrm(ks[6], (DEPTH, D), 0.02),
        'w_in': nrm(ks[7], (DEPTH, D, IN_WIDTH), D ** -0.5),
        'conv_w': nrm(ks[8], (DEPTH, CONV_WIDTH, CONV_CH), CONV_WIDTH ** -0.5),
        'conv_b': nrm(ks[9], (DEPTH, CONV_CH), 0.02),
        'conv_ln_g': 1.0 + nrm(ks[10], (DEPTH, CONV_CH), 0.02),
        'conv_ln_b': nrm(ks[11], (DEPTH, CONV_CH), 0.02),
        'g_attn_out': 1.0 + nrm(ks[12], (DEPTH, ATTN_WIDTH), 0.02),
        'g_conv_out': 1.0 + nrm(ks[13], (DEPTH, CONV_CH), 0.02),
        'w_out': nrm(ks[14], (DEPTH, MIX_WIDTH, D), MIX_WIDTH ** -0.5),
        'g_norm2': 1.0 + nrm(ks[15], (DEPTH, D), 0.02),
        'w_router': nrm(ks[16], (DEPTH, D, N_EXPERTS), D ** -0.5),
        'b_router': nrm(ks[17], (DEPTH, N_EXPERTS), 0.01),
        'w_gate_up': nrm(ks[18], (DEPTH, N_EXPERTS, D, 2 * D_FF), D ** -0.5),
        'b_gate_up': nrm(ks[19], (DEPTH, N_EXPERTS, 2 * D_FF), 0.02),
        'w_down': nrm(ks[20], (DEPTH, N_EXPERTS, D_FF, D), D_FF ** -0.5),
        'b_down': nrm(ks[21], (DEPTH, N_EXPERTS, D), 0.02),
        'g_final': 1.0 + nrm(ks[22], (D,), 0.02),
    }


def reference(x_prompt, x_sample, c_prompt, c_sample, w_ada, b_ada, g_norm1, w_in, conv_w, conv_b,
              conv_ln_g, conv_ln_b, g_attn_out, g_conv_out, w_out, g_norm2, w_router, b_router,
              w_gate_up, b_gate_up, w_down, b_down, g_final):
    y_prompt = encoder_trunk(x_prompt, c_prompt, w_ada, b_ada, g_norm1, w_in, conv_w, conv_b,
                             conv_ln_g, conv_ln_b, g_attn_out, g_conv_out, w_out, g_norm2,
                             w_router, b_router, w_gate_up, b_gate_up, w_down, b_down, g_final)
    y_sample = encoder_trunk(x_sample, c_sample, w_ada, b_ada, g_norm1, w_in, conv_w, conv_b,
                             conv_ln_g, conv_ln_b, g_attn_out, g_conv_out, w_out, g_norm2,
                             w_router, b_router, w_gate_up, b_gate_up, w_down, b_down, g_final)
    return (y_prompt, y_sample)
```

```python
import functools

import numpy as np
import jax
import jax.numpy as jnp
from jax import lax
from jax.experimental import pallas as pl
from jax.experimental.pallas import tpu as pltpu

F32 = jnp.float32
BF16 = jnp.bfloat16
HIGHEST = lax.Precision.HIGHEST

HEAD_DIM = 64
HEAD_PAIR = 2 * HEAD_DIM
CONV_WIDTH = 31
CONV_HALO = 16
DILATED_BRANCHES = ((128, 1), (512, 4), (2048, 16))
HALF_WINDOW = 64
TOP_K = 4
SWIGLU_ALPHA = 1.702
SWIGLU_LIMIT = 7.0
NORM_EPS = 1e-6
NEG_BIG = -1e30

TOKEN_BLOCK = 512
MIX_BLOCK = 256
ATTN_BLOCK = 128
EXPERT_TILE = 512
FF_CHUNK = 512
VMEM_LIMIT = 56 * 1024 * 1024


def _cparams(n_axes):
    return pltpu.CompilerParams(dimension_semantics=("arbitrary",) * n_axes, vmem_limit_bytes=VMEM_LIMIT)


def _rms(x):
    return x * lax.rsqrt(jnp.mean(x * x, axis=-1, keepdims=True) + NORM_EPS)


def _ada_kernel(c_ref, w_ref, b_ref, o_ref):
    c = c_ref[...]
    s = c * jax.nn.sigmoid(c)
    o_ref[...] = jnp.dot(s, w_ref[...], precision=HIGHEST, preferred_element_type=F32) + b_ref[...]


def _ada_call(c_pad, w_ada, b_ada):
    depth, d, n = w_ada.shape
    nsp = c_pad.shape[0]
    tn = d
    return pl.pallas_call(
        _ada_kernel,
        grid=(depth, n // tn),
        in_specs=[
            pl.BlockSpec((nsp, d), lambda l, j: (0, 0)),
            pl.BlockSpec((None, d, tn), lambda l, j: (l, 0, j)),
            pl.BlockSpec((None, 1, tn), lambda l, j: (l, 0, j)),
        ],
        out_specs=pl.BlockSpec((None, nsp, tn), lambda l, j: (l, 0, j)),
        out_shape=jax.ShapeDtypeStruct((depth, nsp, n), F32),
        compiler_params=_cparams(2),
        name="ada_mod",
    )(c_pad, w_ada, b_ada.reshape(depth, 1, n))


def _inproj_kernel(seq_ref, *refs, attn_w, conv_ch, combine):
    del seq_ref
    if combine:
        (x_ref, y0, y1, y2, y3, gt_ref, g_ref, sc_ref, sh_ref, w_ref,
         xo_ref, q_ref, k_ref, v_ref, u_ref) = refs
        x = x_ref[...] + gt_ref[...] * ((y0[...] + y1[...]) + (y2[...] + y3[...]))
        xo_ref[...] = x
    else:
        x_ref, g_ref, sc_ref, sh_ref, w_ref, q_ref, k_ref, v_ref, u_ref = refs
        x = x_ref[...]
    h = _rms(x) * g_ref[...]
    h = h * (1.0 + sc_ref[...]) + sh_ref[...]
    proj = jnp.dot(h.astype(BF16), w_ref[...], preferred_element_type=F32)
    a = attn_w
    q_ref[...] = (proj[:, :a] * (HEAD_DIM ** -0.5)).astype(BF16)
    k_ref[...] = proj[:, a:2 * a].astype(BF16)
    v_ref[...] = proj[:, 2 * a:3 * a].astype(BF16)
    cv = proj[:, 3 * a:3 * a + conv_ch]
    cg = proj[:, 3 * a + conv_ch:]
    u_ref[...] = (cv * jax.nn.sigmoid(cg)).astype(BF16)


def _mod_spec(comp, nsp, d):
    return pl.BlockSpec((None, 1, d), lambda i, seq: (comp * nsp + seq[i], 0, 0))


def _inproj_call(blk_seq, x, y4, mod_l, mod_prev, g_norm, w_in_bf, attn_w, conv_ch, nsp):
    t, d = x.shape
    tb = TOKEN_BLOCK
    nb = t // tb
    combine = y4 is not None
    row = pl.BlockSpec((tb, d), lambda i, seq: (i, 0))
    in_specs = [row]
    args = [x]
    if combine:
        for k in range(TOP_K):
            in_specs.append(pl.BlockSpec((tb, d), lambda i, seq, k=k: (k * nb + i, 0)))
            args.append(y4)
        in_specs.append(_mod_spec(5, nsp, d))
        args.append(mod_prev)
    in_specs += [
        pl.BlockSpec((1, d), lambda i, seq: (0, 0)),
        _mod_spec(1, nsp, d),
        _mod_spec(0, nsp, d),
        pl.BlockSpec(w_in_bf.shape, lambda i, seq: (0, 0)),
    ]
    args += [g_norm.reshape(1, d), mod_l, mod_l, w_in_bf]
    out_shapes = []
    out_specs = []
    if combine:
        out_shapes.append(jax.ShapeDtypeStruct((t, d), F32))
        out_specs.append(row)
    for width in (attn_w, attn_w, attn_w, conv_ch):
        out_shapes.append(jax.ShapeDtypeStruct((t, width), BF16))
        out_specs.append(pl.BlockSpec((tb, width), lambda i, seq: (i, 0)))
    outs = pl.pallas_call(
        functools.partial(_inproj_kernel, attn_w=attn_w, conv_ch=conv_ch, combine=combine),
        grid_spec=pltpu.PrefetchScalarGridSpec(
            num_scalar_prefetch=1, grid=(nb,), in_specs=in_specs, out_specs=out_specs),
        out_shape=out_shapes,
        compiler_params=_cparams(1),
        name="in_proj",
    )(blk_seq, *args)
    if combine:
        return outs[0], outs[1:]
    return x, outs


def _attn_kernel(lo_ref, hi_ref, q_ref, kp_ref, kc_ref, kn_ref, vp_ref, vc_ref, vn_ref, bias_ref,
                 o_ref, lse_ref, *, n_pairs):
    i = pl.program_id(1)
    lb = q_ref.shape[0]
    kw = lb + 2 * HALF_WINDOW
    kpos = i * lb - HALF_WINDOW + lax.broadcasted_iota(jnp.int32, (1, kw), 1)
    col_ok = (kpos >= lo_ref[i]) & (kpos < hi_ref[i])
    lane = lax.broadcasted_iota(jnp.int32, (1, HEAD_PAIR), 1)
    for j in range(n_pairs):
        sl = slice(HEAD_PAIR * j, HEAD_PAIR * (j + 1))
        qp = q_ref[:, sl]
        kwin = jnp.concatenate([kp_ref[:, sl], kc_ref[:, sl], kn_ref[:, sl]], axis=0)
        vwin = jnp.concatenate([vp_ref[:, sl], vc_ref[:, sl], vn_ref[:, sl]], axis=0)
        o_pair = None
        lse_pair = None
        for b in range(2):
            in_head = (lane >= HEAD_DIM * b) & (lane < HEAD_DIM * (b + 1))
            head_mask = jnp.where(in_head, 1.0, 0.0).astype(BF16)
            s = lax.dot_general(qp * head_mask, kwin, (((1,), (1,)), ((), ())),
                                preferred_element_type=F32)
            s = s + bias_ref[2 * j + b]
            s = jnp.where(col_ok, s, NEG_BIG)
            m = jnp.max(s, axis=-1, keepdims=True)
            p = jnp.exp(s - m)
            den = jnp.sum(p, axis=-1, keepdims=True)
            pv = jnp.dot(p.astype(BF16), vwin, preferred_element_type=F32)
            o_b = pv * (1.0 / den)
            lse_b = jnp.broadcast_to(m + jnp.log(den), (lb, HEAD_PAIR))
            if b == 0:
                o_pair, lse_pair = o_b, lse_b
            else:
                o_pair = jnp.where(in_head, o_b, o_pair)
                lse_pair = jnp.where(in_head, lse_b, lse_pair)
        o_ref[:, sl] = o_pair.astype(BF16)
        lse_ref[:, sl] = lse_pair


def _attn_bias(dilation, n_heads, lb):
    kw = lb + 2 * HALF_WINDOW
    rel = (np.arange(kw)[None, :] - HALF_WINDOW) - np.arange(lb)[:, None]
    slopes = 2.0 ** (-8.0 * np.arange(1, n_heads + 1, dtype=np.float64) / n_heads)
    bias = -slopes[:, None, None] * (dilation * np.abs(rel))[None].astype(np.float64)
    bias = np.where((np.abs(rel) <= HALF_WINDOW)[None], bias, NEG_BIG)
    return jnp.asarray(bias.astype(np.float32))


def _attn_call(q, k, v, dilation, seq_bounds):
    t, a = q.shape
    lb = ATTN_BLOCK
    n_heads = a // HEAD_DIM
    rows = t // dilation
    nb = rows // lb
    sub = lb // HALF_WINDOW
    n_halo_blocks = rows // HALF_WINDOW
    lo = np.zeros((nb,), np.int32)
    hi = np.zeros((nb,), np.int32)
    for (t_lo, t_hi) in seq_bounds:
        assert t_lo % (dilation * lb) == 0 and t_hi % (dilation * lb) == 0
        lo[t_lo // dilation // lb:t_hi // dilation // lb] = t_lo // dilation
        hi[t_lo // dilation // lb:t_hi // dilation // lb] = t_hi // dilation
    view = lambda z: z.reshape(rows, dilation * a)
    cur = pl.BlockSpec((lb, a), lambda r, i, lo, hi: (i, r))
    prev = pl.BlockSpec((HALF_WINDOW, a), lambda r, i, lo, hi: (jnp.maximum(i * sub - 1, 0), r))
    nxt = pl.BlockSpec((HALF_WINDOW, a),
                       lambda r, i, lo, hi: (jnp.minimum((i + 1) * sub, n_halo_blocks - 1), r))
    bias = _attn_bias(dilation, n_heads, lb)
    o, lse = pl.pallas_call(
        functools.partial(_attn_kernel, n_pairs=a // HEAD_PAIR),
        grid_spec=pltpu.PrefetchScalarGridSpec(
            num_scalar_prefetch=2, grid=(dilation, nb),
            in_specs=[cur, prev, cur, nxt, prev, cur, nxt,
                      pl.BlockSpec(bias.shape, lambda r, i, lo, hi: (0, 0, 0))],
            out_specs=[cur, cur]),
        out_shape=[jax.ShapeDtypeStruct((rows, dilation * a), BF16),
                   jax.ShapeDtypeStruct((rows, dilation * a), F32)],
        compiler_params=_cparams(2),
        name=f"attn_d{dilation}",
    )(jnp.asarray(lo), jnp.asarray(hi), view(q), view(k), view(k), view(k), view(v), view(v), view(v), bias)
    return o.reshape(t, a), lse.reshape(t, a)


def _mix_kernel(seq_ref, first_ref, last_ref, x_ref, up_ref, uc_ref, un_ref,
                o1_ref, l1_ref, o2_ref, l2_ref, o3_ref, l3_ref,
                cw_ref, cb_ref, lng_ref, lnb_ref, ga_ref, gc_ref, wo_ref, gt_ref, xo_ref, *, attn_w):
    del seq_ref
    i = pl.program_id(0)
    tb = x_ref.shape[0]
    la, lb_, lc = l1_ref[...], l2_ref[...], l3_ref[...]
    m = jnp.maximum(jnp.maximum(la, lb_), lc)
    ea, eb, ec = jnp.exp(la - m), jnp.exp(lb_ - m), jnp.exp(lc - m)
    inv = 1.0 / (ea + eb + ec)
    attn = (ea * o1_ref[...].astype(F32) + eb * o2_ref[...].astype(F32) + ec * o3_ref[...].astype(F32)) * inv
    attn_n = _rms(attn) * ga_ref[...]

    keep_prev = jnp.where(first_ref[i] == 1, 0.0, 1.0)
    keep_next = jnp.where(last_ref[i] == 1, 0.0, 1.0)
    win = jnp.concatenate([up_ref[...].astype(F32) * keep_prev, uc_ref[...].astype(F32),
                           un_ref[...].astype(F32) * keep_next], axis=0)
    n = tb + 2 * CONV_HALO
    acc = None
    for rho in range(8):
        shifted = win if rho == 0 else pltpu.roll(win, n - rho, axis=0)
        for blk in range(2 * CONV_HALO // 8):
            tap = 8 * blk + rho - (CONV_HALO - CONV_WIDTH // 2)
            if 0 <= tap < CONV_WIDTH:
                term = cw_ref[tap:tap + 1, :] * shifted[8 * blk:8 * blk + tb]
                acc = term if acc is None else acc + term
    conv = acc + cb_ref[...]
    mu = jnp.mean(conv, axis=-1, keepdims=True)
    xc = conv - mu
    y = xc * lax.rsqrt(jnp.mean(xc * xc, axis=-1, keepdims=True) + NORM_EPS) * lng_ref[...] + lnb_ref[...]
    y = y * jax.nn.sigmoid(y)
    conv_n = _rms(y) * gc_ref[...]

    out = jnp.dot(attn_n.astype(BF16), wo_ref[:attn_w, :], preferred_element_type=F32)
    out = out + jnp.dot(conv_n.astype(BF16), wo_ref[attn_w:, :], preferred_element_type=F32)
    xo_ref[...] = x_ref[...] + gt_ref[...] * out


def _mix_call(blk_seq, first, last, x, u, branches, mod_l, lw, nsp):
    t, d = x.shape
    c = u.shape[1]
    a = branches[0][0].shape[1]
    tb = MIX_BLOCK
    nb = t // tb
    sub = tb // CONV_HALO
    n_halo_blocks = t // CONV_HALO
    row = lambda w: pl.BlockSpec((tb, w), lambda i, *_: (i, 0))
    vec = lambda w: pl.BlockSpec((1, w), lambda i, *_: (0, 0))
    in_specs = [
        row(d),
        pl.BlockSpec((CONV_HALO, c), lambda i, *_: (jnp.maximum(i * sub - 1, 0), 0)),
        row(c),
        pl.BlockSpec((CONV_HALO, c), lambda i, *_: (jnp.minimum((i + 1) * sub, n_halo_blocks - 1), 0)),
    ]
    args = [x, u, u, u]
    for o, lse in branches:
        in_specs += [row(a), row(a)]
        args += [o, lse]
    in_specs += [
        pl.BlockSpec((CONV_WIDTH, c), lambda i, *_: (0, 0)),
        vec(c), vec(c), vec(c), vec(a), vec(c),
        pl.BlockSpec((a + c, d), lambda i, *_: (0, 0)),
        pl.BlockSpec((None, 1, d), lambda i, seq, *_: (2 * nsp + seq[i], 0, 0)),
    ]
    args += [lw["conv_w"], lw["conv_b"].reshape(1, c), lw["conv_ln_g"].reshape(1, c),
             lw["conv_ln_b"].reshape(1, c), lw["g_attn_out"].reshape(1, a), lw["g_conv_out"].reshape(1, c),
             lw["w_out_bf"], mod_l]
    return pl.pallas_call(
        functools.partial(_mix_kernel, attn_w=a),
        grid_spec=pltpu.PrefetchScalarGridSpec(
            num_scalar_prefetch=3, grid=(nb,), in_specs=in_specs, out_specs=row(d)),
        out_shape=jax.ShapeDtypeStruct((t, d), F32),
        compiler_params=_cparams(1),
        name="mix_out",
    )(blk_seq, first, last, *args)


def _router_kernel(seq_ref, x_ref, g_ref, sc_ref, sh_ref, wr_ref, br_ref, h_ref, idx_ref, wt_ref, *, n_experts):
    del seq_ref
    h = _rms(x_ref[...]) * g_ref[...]
    h = h * (1.0 + sc_ref[...]) + sh_ref[...]
    h_ref[...] = h
    logits = jnp.dot(h, wr_ref[...], precision=HIGHEST, preferred_element_type=F32) + br_ref[...]
    lt = logits.T[:n_experts]
    eio = lax.broadcasted_iota(jnp.int32, lt.shape, 0)
    vals, idxs = [], []
    for _ in range(TOP_K):
        m = jnp.max(lt, axis=0, keepdims=True)
        ix = jnp.min(jnp.where(lt == m, eio, n_experts), axis=0, keepdims=True)
        vals.append(m)
        idxs.append(ix)
        lt = jnp.where(eio == ix, -jnp.inf, lt)
    es = [jnp.exp(v - vals[0]) for v in vals]
    inv = 1.0 / (es[0] + es[1] + es[2] + es[3])
    for k in range(TOP_K):
        idx_ref[k:k + 1, :] = idxs[k]
        wt_ref[k:k + 1, :] = es[k] * inv


def _router_call(blk_seq, x, mod_l, g_norm, w_router_pad, b_router_pad, n_experts, nsp):
    t, d = x.shape
    tb = TOKEN_BLOCK
    ep = w_router_pad.shape[1]
    row = pl.BlockSpec((tb, d), lambda i, seq: (i, 0))
    lane_row = pl.BlockSpec((TOP_K, tb), lambda i, seq: (0, i))
    return pl.pallas_call(
        functools.partial(_router_kernel, n_experts=n_experts),
        grid_spec=pltpu.PrefetchScalarGridSpec(
            num_scalar_prefetch=1, grid=(t // tb,),
            in_specs=[row, pl.BlockSpec((1, d), lambda i, seq: (0, 0)), _mod_spec(4, nsp, d), _mod_spec(3, nsp, d),
                      pl.BlockSpec((d, ep), lambda i, seq: (0, 0)), pl.BlockSpec((1, ep), lambda i, seq: (0, 0))],
            out_specs=[row, lane_row, lane_row]),
        out_shape=[jax.ShapeDtypeStruct((t, d), F32), jax.ShapeDtypeStruct((TOP_K, t), jnp.int32),
                   jax.ShapeDtypeStruct((TOP_K, t), F32)],
        compiler_params=_cparams(1),
        name="router",
    )(blk_seq, x, g_norm.reshape(1, d), mod_l, mod_l, w_router_pad, b_router_pad)


def _group_by_expert(idx, wt, n_experts, tm):
    k, t = idx.shape
    na = k * t
    p_max = na + n_experts * tm
    e_flat = idx.reshape(na)
    w_flat = wt.reshape(na)
    order = jnp.argsort(e_flat, stable=True).astype(jnp.int32)
    e_sorted = e_flat[order]
    gstart = jnp.searchsorted(e_sorted, jnp.arange(n_experts + 1, dtype=jnp.int32), side="left").astype(jnp.int32)
    counts = gstart[1:] - gstart[:-1]
    pcounts = ((counts + tm - 1) // tm) * tm
    pstart = jnp.concatenate([jnp.zeros((1,), jnp.int32), jnp.cumsum(pcounts).astype(jnp.int32)])
    pp = jnp.arange(p_max, dtype=jnp.int32)
    e_pp = jnp.clip(jnp.searchsorted(pstart, pp, side="right").astype(jnp.int32) - 1, 0, n_experts - 1)
    rank = pp - pstart[e_pp]
    valid = rank < counts[e_pp]
    a_id = order[jnp.clip(gstart[e_pp] + rank, 0, na - 1)]
    tok = jnp.where(valid, a_id % t, 0)
    dest = jnp.where(valid, a_id, na + pp % tm)
    wgt = jnp.where(valid, w_flat[a_id], 0.0)
    nt = p_max // tm
    n_used = (pstart[n_experts] // tm).reshape(1)
    tile_e = e_pp.reshape(nt, tm)[:, 0]
    return tok.reshape(nt, 1, tm), dest.reshape(nt, 1, tm), wgt.reshape(p_max, 1), tile_e, n_used


def _expert_kernel(te_ref, nu_ref, tokc_ref, tokn_ref, dst_ref, wgt_ref, h_hbm,
                   wgu_ref, bgu_ref, wd_ref, bd_ref, y_hbm, xbuf, ybuf, gsem, ssem):
    del te_ref
    j = pl.program_id(0)
    n_used = nu_ref[0]
    slot = j % 2
    tm = xbuf.shape[1]
    d_ff = wd_ref.shape[0]

    def gather_rows(tok_ref, s):
        def body(r, carry):
            pltpu.make_async_copy(h_hbm.at[pl.ds(tok_ref[0, r], 1)], xbuf.at[s, pl.ds(r, 1)], gsem.at[s]).start()
            return carry
        lax.fori_loop(0, tm, body, 0, unroll=8)

    def wait_rows(buf, sem, s):
        pltpu.make_async_copy(buf.at[s], buf.at[s], sem.at[s]).wait()

    @pl.when(j == 0)
    def _():
        gather_rows(tokc_ref, 0)
        ybuf[1] = jnp.zeros(ybuf.shape[1:], F32)
        trash = pltpu.make_async_copy(ybuf.at[1], y_hbm.at[pl.ds(y_hbm.shape[0] - tm, tm)], ssem.at[1])
        trash.start()
        trash.wait()

    @pl.when(j + 1 < n_used)
    def _():
        gather_rows(tokn_ref, 1 - slot)

    @pl.when(j < n_used)
    def _():
        wait_rows(xbuf, gsem, slot)

        @pl.when(j >= 2)
        def _():
            wait_rows(ybuf, ssem, slot)

        xb = xbuf[slot].astype(BF16)
        acc = None
        for c in range(d_ff // FF_CHUNK):
            gs = slice(c * FF_CHUNK, (c + 1) * FF_CHUNK)
            us = slice(d_ff + c * FF_CHUNK, d_ff + (c + 1) * FF_CHUNK)
            gate = jnp.dot(xb, wgu_ref[:, gs], preferred_element_type=F32) + bgu_ref[:, gs]
            up = jnp.dot(xb, wgu_ref[:, us], preferred_element_type=F32) + bgu_ref[:, us]
            gate = jnp.minimum(gate, SWIGLU_LIMIT)
            up = jnp.clip(up, -SWIGLU_LIMIT, SWIGLU_LIMIT)
            act = (up + 1.0) * gate * jax.nn.sigmoid(SWIGLU_ALPHA * gate)
            part = jnp.dot(act.astype(BF16), wd_ref[gs, :], preferred_element_type=F32)
            acc = part if acc is None else acc + part
        ybuf[slot] = (acc + bd_ref[...]) * wgt_ref[...]

        def scatter_body(r, carry):
            pltpu.make_async_copy(ybuf.at[slot, pl.ds(r, 1)], y_hbm.at[pl.ds(dst_ref[0, r], 1)], ssem.at[slot]).start()
            return carry
        lax.fori_loop(0, tm, scatter_body, 0, unroll=8)

    @pl.when(j == n_used - 1)
    def _():
        wait_rows(ybuf, ssem, slot)

        @pl.when(n_used >= 2)
        def _():
            wait_rows(ybuf, ssem, 1 - slot)


def _expert_call(h, groups, wgu_bf, bgu, wd_bf, bd, tm):
    tok, dest, wgt, tile_e, n_used = groups
    t, d = h.shape
    n_experts, _, two_f = wgu_bf.shape
    d_ff = two_f // 2
    nt = tok.shape[0]
    na = TOP_K * t
    smem_row = lambda fn: pl.BlockSpec((None, 1, tm), fn, memory_space=pltpu.SMEM)
    return pl.pallas_call(
        _expert_kernel,
        grid_spec=pltpu.PrefetchScalarGridSpec(
            num_scalar_prefetch=2, grid=(nt,),
            in_specs=[
                smem_row(lambda j, te, nu: (j, 0, 0)),
                smem_row(lambda j, te, nu: (jnp.minimum(j + 1, nt - 1), 0, 0)),
                smem_row(lambda j, te, nu: (j, 0, 0)),
                pl.BlockSpec((tm, 1), lambda j, te, nu: (j, 0)),
                pl.BlockSpec(memory_space=pl.ANY),
                pl.BlockSpec((None, d, two_f), lambda j, te, nu: (te[j], 0, 0)),
                pl.BlockSpec((None, 1, two_f), lambda j, te, nu: (te[j], 0, 0)),
                pl.BlockSpec((None, d_ff, d), lambda j, te, nu: (te[j], 0, 0)),
                pl.BlockSpec((None, 1, d), lambda j, te, nu: (te[j], 0, 0)),
            ],
            out_specs=pl.BlockSpec(memory_space=pl.ANY),
            scratch_shapes=[pltpu.VMEM((2, tm, d), F32), pltpu.VMEM((2, tm, d), F32),
                            pltpu.SemaphoreType.DMA((2,)), pltpu.SemaphoreType.DMA((2,))]),
        out_shape=jax.ShapeDtypeStruct((na + tm, d), F32),
        compiler_params=_cparams(1),
        name="expert_ffn",
    )(tile_e, n_used, tok, tok, dest, wgt, h, wgu_bf, bgu.reshape(n_experts, 1, two_f), wd_bf,
      bd.reshape(n_experts, 1, d))


def _final_kernel(seq_ref, x_ref, y0, y1, y2, y3, gt_ref, g_ref, o_ref):
    del seq_ref
    x = x_ref[...] + gt_ref[...] * ((y0[...] + y1[...]) + (y2[...] + y3[...]))
    o_ref[...] = _rms(x) * g_ref[...]


def _final_call(blk_seq, x, y4, mod_prev, g_final, nsp):
    t, d = x.shape
    tb = TOKEN_BLOCK
    nb = t // tb
    row = pl.BlockSpec((tb, d), lambda i, seq: (i, 0))
    in_specs = [row] + [pl.BlockSpec((tb, d), lambda i, seq, k=k: (k * nb + i, 0)) for k in range(TOP_K)]
    in_specs += [_mod_spec(5, nsp, d), pl.BlockSpec((1, d), lambda i, seq: (0, 0))]
    return pl.pallas_call(
        _final_kernel,
        grid_spec=pltpu.PrefetchScalarGridSpec(num_scalar_prefetch=1, grid=(nb,), in_specs=in_specs, out_specs=row),
        out_shape=jax.ShapeDtypeStruct((t, d), F32),
        compiler_params=_cparams(1),
        name="final_norm",
    )(blk_seq, x, y4, y4, y4, y4, mod_prev, g_final.reshape(1, d))


def _block_tables(seq_lens, tb):
    seq, first, last = [], [], []
    for s, n in enumerate(seq_lens):
        assert n % tb == 0
        nblk = n // tb
        seq += [s] * nblk
        first += [1] + [0] * (nblk - 1)
        last += [0] * (nblk - 1) + [1]
    as_i32 = lambda z: jnp.asarray(np.asarray(z, np.int32))
    return as_i32(seq), as_i32(first), as_i32(last)


def kernel(x_prompt, x_sample, c_prompt, c_sample, w_ada, b_ada, g_norm1, w_in, conv_w, conv_b, conv_ln_g,
           conv_ln_b, g_attn_out, g_conv_out, w_out, g_norm2, w_router, b_router, w_gate_up, b_gate_up,
           w_down, b_down, g_final):
    bp, sp, d = x_prompt.shape
    bs, ss, _ = x_sample.shape
    depth = w_ada.shape[0]
    conv_ch = conv_w.shape[2]
    attn_w = (w_in.shape[2] - 2 * conv_ch) // 3
    n_experts = w_router.shape[2]
    tp, ts = bp * sp, bs * ss
    t = tp + ts
    seq_lens = [sp] * bp + [ss] * bs
    nseq = len(seq_lens)
    nsp = -(-nseq // 8) * 8
    seq_bounds = []
    start = 0
    for n in seq_lens:
        seq_bounds.append((start, start + n))
        start += n

    x = jnp.concatenate([x_prompt.reshape(tp, d), x_sample.reshape(ts, d)], axis=0)
    c_all = jnp.concatenate([c_prompt, c_sample, jnp.zeros((nsp - nseq, d), F32)], axis=0)
    mod = _ada_call(c_all, w_ada, b_ada)
    mod = mod.reshape(depth, nsp, 6, d).transpose(0, 2, 1, 3).reshape(depth, 6 * nsp, 1, d)

    seq_tok, _, _ = _block_tables(seq_lens, TOKEN_BLOCK)
    seq_mix, first_mix, last_mix = _block_tables(seq_lens, MIX_BLOCK)

    ep = -(-n_experts // 128) * 128
    w_router_pad = jnp.pad(w_router, ((0, 0), (0, 0), (0, ep - n_experts)))
    b_router_pad = jnp.pad(b_router, ((0, 0), (0, ep - n_experts))).reshape(depth, 1, ep)
    w_in_bf = w_in.astype(BF16)
    w_out_bf = w_out.astype(BF16)
    wgu_bf = w_gate_up.astype(BF16)
    wd_bf = w_down.astype(BF16)

    y4 = None
    for l in range(depth):
        x, (q, k, v, u) = _inproj_call(seq_tok, x, y4, mod[l], mod[l - 1] if l else None, g_norm1[l],
                                       w_in_bf[l], attn_w, conv_ch, nsp)
        branches = [_attn_call(q, k, v, dilation, seq_bounds) for _, dilation in DILATED_BRANCHES]
        lw = dict(conv_w=conv_w[l], conv_b=conv_b[l], conv_ln_g=conv_ln_g[l], conv_ln_b=conv_ln_b[l],
                  g_attn_out=g_attn_out[l], g_conv_out=g_conv_out[l], w_out_bf=w_out_bf[l])
        x = _mix_call(seq_mix, first_mix, last_mix, x, u, branches, mod[l], lw, nsp)
        h, idx, wt = _router_call(seq_tok, x, mod[l], g_norm2[l], w_router_pad[l], b_router_pad[l], n_experts, nsp)
        groups = _group_by_expert(idx, wt, n_experts, EXPERT_TILE)
        y4 = _expert_call(h, groups, wgu_bf[l], b_gate_up[l], wd_bf[l], b_down[l], EXPERT_TILE)
    out = _final_call(seq_tok, x, y4, mod[depth - 1], g_final, nsp)
    return out[:tp].reshape(bp, sp, d), out[tp:].reshape(bs, ss, d)
```

```python
import functools

import numpy as np
import jax
import jax.numpy as jnp
from jax import lax
from jax.experimental import pallas as pl
from jax.experimental.pallas import tpu as pltpu

F32 = jnp.float32
BF16 = jnp.bfloat16
HIGHEST = lax.Precision.HIGHEST

LANES = 128
HEAD_DIM = 64
HEAD_PAIR = 2 * HEAD_DIM
CONV_WIDTH = 31
CONV_HALO = 16
DILATED_BRANCHES = ((128, 1), (512, 4), (2048, 16))
HALF_WINDOW = 64
TOP_K = 4
SWIGLU_ALPHA = 1.702
SWIGLU_LIMIT = 7.0
NORM_EPS = 1e-6
NEG_BIG = -1e30

TOKEN_BLOCK = 512
MIX_BLOCK = 256
ATTN_BLOCK = 128
EXPERT_TILE = 512
FF_CHUNK = 256
VMEM_LIMIT = 56 * 1024 * 1024


def _cparams(n_axes):
    return pltpu.CompilerParams(dimension_semantics=("arbitrary",) * n_axes, vmem_limit_bytes=VMEM_LIMIT)


def _rms(x):
    return x * lax.rsqrt(jnp.mean(x * x, axis=-1, keepdims=True) + NORM_EPS)


def _load_tile_major(ref, n_rows, s_tiles):
    return jnp.concatenate([ref[pl.ds(s, n_rows, stride=s_tiles), :] for s in range(s_tiles)], axis=1)


def _store_tile_major(ref, val, s_tiles):
    n_rows = val.shape[0]
    for s in range(s_tiles):
        ref[pl.ds(s, n_rows, stride=s_tiles), :] = val[:, LANES * s:LANES * (s + 1)]


def _ada_kernel(c_ref, w_ref, b_ref, o_ref):
    c = c_ref[...]
    s = c * jax.nn.sigmoid(c)
    o_ref[...] = jnp.dot(s, w_ref[...], precision=HIGHEST, preferred_element_type=F32) + b_ref[...]


def _ada_call(c_pad, w_ada, b_ada):
    depth, d, n = w_ada.shape
    nsp = c_pad.shape[0]
    tn = d
    return pl.pallas_call(
        _ada_kernel,
        grid=(depth, n // tn),
        in_specs=[
            pl.BlockSpec((nsp, d), lambda l, j: (0, 0)),
            pl.BlockSpec((None, d, tn), lambda l, j: (l, 0, j)),
            pl.BlockSpec((None, 1, tn), lambda l, j: (l, 0, j)),
        ],
        out_specs=pl.BlockSpec((None, nsp, tn), lambda l, j: (l, 0, j)),
        out_shape=jax.ShapeDtypeStruct((depth, nsp, n), F32),
        compiler_params=_cparams(2),
        name="ada_mod",
    )(c_pad, w_ada, b_ada.reshape(depth, 1, n))


def _inproj_kernel(seq_ref, *refs, attn_w, conv_ch, combine, dilations):
    del seq_ref
    n_reg = 3 * (len(dilations) - 1)
    if combine:
        x_ref, y0, y1, y2, y3, gt_ref, g_ref, sc_ref, sh_ref, w_ref, xo_ref = refs[:11]
        rest = refs[11:]
        tb, d = x_ref.shape
        s_tiles = d // LANES
        ysum = (_load_tile_major(y0, tb, s_tiles) + _load_tile_major(y1, tb, s_tiles)) + (
            _load_tile_major(y2, tb, s_tiles) + _load_tile_major(y3, tb, s_tiles))
        x = x_ref[...] + gt_ref[...] * ysum
        xo_ref[...] = x
    else:
        x_ref, g_ref, sc_ref, sh_ref, w_ref = refs[:5]
        rest = refs[5:]
        tb = x_ref.shape[0]
        x = x_ref[...]
    q_ref, k_ref, v_ref = rest[:3]
    regrouped = rest[3:3 + n_reg]
    u_ref, pbuf = rest[3 + n_reg:]
    h = _rms(x) * g_ref[...]
    h = h * (1.0 + sc_ref[...]) + sh_ref[...]
    proj = jnp.dot(h.astype(BF16), w_ref[...], preferred_element_type=F32)
    a = attn_w
    q = proj[:, :a] * (HEAD_DIM ** -0.5)
    q_ref[0] = q.astype(BF16)
    k_ref[0] = proj[:, a:2 * a].astype(BF16)
    v_ref[0] = proj[:, 2 * a:3 * a].astype(BF16)
    cv = proj[:, 3 * a:3 * a + conv_ch]
    cg = proj[:, 3 * a + conv_ch:]
    u_ref[...] = (cv * jax.nn.sigmoid(cg)).astype(BF16)
    tiles_per = a // LANES
    for c in range(3 * tiles_per):
        cols = slice(LANES * c, LANES * (c + 1))
        pbuf[c] = q[:, cols] if c < tiles_per else proj[:, cols]
    for bi, dil in enumerate(dilations[1:]):
        for r in range(dil):
            for c in range(3 * tiles_per):
                dst = regrouped[3 * bi + c // tiles_per]
                cols = slice(LANES * (c % tiles_per), LANES * (c % tiles_per + 1))
                dst[r, :, cols] = pbuf[c, pl.ds(r, tb // dil, stride=dil), :].astype(BF16)


def _mod_spec(comp, nsp, d):
    return pl.BlockSpec((None, 1, d), lambda i, seq: (comp * nsp + seq[i], 0, 0))


def _inproj_call(blk_seq, x, y4, mod_l, mod_prev, g_norm, w_in_bf, attn_w, conv_ch, nsp):
    t, d = x.shape
    tb = TOKEN_BLOCK
    nb = t // tb
    s_tiles = d // LANES
    combine = y4 is not None
    dilations = tuple(dil for _, dil in DILATED_BRANCHES)
    assert dilations[0] == 1
    row = pl.BlockSpec((tb, d), lambda i, seq: (i, 0))
    in_specs = [row]
    args = [x]
    if combine:
        for k in range(TOP_K):
            in_specs.append(pl.BlockSpec((tb * s_tiles, LANES), lambda i, seq, k=k: (k * nb + i, 0)))
            args.append(y4)
        in_specs.append(_mod_spec(5, nsp, d))
        args.append(mod_prev)
    in_specs += [
        pl.BlockSpec((1, d), lambda i, seq: (0, 0)),
        _mod_spec(1, nsp, d),
        _mod_spec(0, nsp, d),
        pl.BlockSpec(w_in_bf.shape, lambda i, seq: (0, 0)),
    ]
    args += [g_norm.reshape(1, d), mod_l, mod_l, w_in_bf]
    out_shapes = []
    out_specs = []
    if combine:
        out_shapes.append(jax.ShapeDtypeStruct((t, d), F32))
        out_specs.append(row)
    for dil in dilations:
        for _ in range(3):
            out_shapes.append(jax.ShapeDtypeStruct((dil, t // dil, attn_w), BF16))
            out_specs.append(pl.BlockSpec((dil, tb // dil, attn_w), lambda i, seq: (0, i, 0)))
    out_shapes.append(jax.ShapeDtypeStruct((t, conv_ch), BF16))
    out_specs.append(pl.BlockSpec((tb, conv_ch), lambda i, seq: (i, 0)))
    outs = pl.pallas_call(
        functools.partial(_inproj_kernel, attn_w=attn_w, conv_ch=conv_ch, combine=combine, dilations=dilations),
        grid_spec=pltpu.PrefetchScalarGridSpec(
            num_scalar_prefetch=1, grid=(nb,), in_specs=in_specs, out_specs=out_specs,
            scratch_shapes=[pltpu.VMEM((3 * attn_w // LANES, tb, LANES), F32)]),
        out_shape=out_shapes,
        compiler_params=_cparams(1),
        name="in_proj",
    )(blk_seq, *args)
    if combine:
        x, outs = outs[0], outs[1:]
    qkv = [tuple(outs[3 * b:3 * b + 3]) for b in range(len(dilations))]
    return x, qkv, outs[-1]


def _attn_kernel(lo_ref, hi_ref, q_ref, kp_ref, kc_ref, kn_ref, vp_ref, vc_ref, vn_ref, bias_ref,
                 o_ref, lse_ref, *, n_pairs):
    i = pl.program_id(1)
    lb = q_ref.shape[0]
    kw = lb + 2 * HALF_WINDOW
    kpos = i * lb - HALF_WINDOW + lax.broadcasted_iota(jnp.int32, (1, kw), 1)
    col_ok = (kpos >= lo_ref[i]) & (kpos < hi_ref[i])
    lane = lax.broadcasted_iota(jnp.int32, (1, HEAD_PAIR), 1)
    for j in range(n_pairs):
        sl = slice(HEAD_PAIR * j, HEAD_PAIR * (j + 1))
        qp = q_ref[:, sl]
        kwin = jnp.concatenate([kp_ref[:, sl], kc_ref[:, sl], kn_ref[:, sl]], axis=0)
        vwin = jnp.concatenate([vp_ref[:, sl], vc_ref[:, sl], vn_ref[:, sl]], axis=0)
        o_pair = None
        lse_pair = None
        for b in range(2):
            in_head = (lane >= HEAD_DIM * b) & (lane < HEAD_DIM * (b + 1))
            head_mask = jnp.where(in_head, 1.0, 0.0).astype(BF16)
            s = lax.dot_general(qp * head_mask, kwin, (((1,), (1,)), ((), ())),
                                preferred_element_type=F32)
            s = s + bias_ref[2 * j + b]
            s = jnp.where(col_ok, s, NEG_BIG)
            m = jnp.max(s, axis=-1, keepdims=True)
            p = jnp.exp(s - m)
            den = jnp.sum(p, axis=-1, keepdims=True)
            pv = jnp.dot(p.astype(BF16), vwin, preferred_element_type=F32)
            o_b = pv * (1.0 / den)
            lse_b = jnp.broadcast_to(m + jnp.log(den), (lb, HEAD_PAIR))
            if b == 0:
                o_pair, lse_pair = o_b, lse_b
            else:
                o_pair = jnp.where(in_head, o_b, o_pair)
                lse_pair = jnp.where(in_head, lse_b, lse_pair)
        o_ref[:, sl] = o_pair.astype(BF16)
        lse_ref[:, sl] = lse_pair


def _attn_bias(dilation, n_heads, lb):
    kw = lb + 2 * HALF_WINDOW
    rel = (np.arange(kw)[None, :] - HALF_WINDOW) - np.arange(lb)[:, None]
    slopes = 2.0 ** (-8.0 * np.arange(1, n_heads + 1, dtype=np.float64) / n_heads)
    bias = -slopes[:, None, None] * (dilation * np.abs(rel))[None].astype(np.float64)
    bias = np.where((np.abs(rel) <= HALF_WINDOW)[None], bias, NEG_BIG)
    return jnp.asarray(bias.astype(np.float32))


def _attn_call(q, k, v, seq_bounds):
    dilation, rows, a = q.shape
    lb = ATTN_BLOCK
    n_heads = a // HEAD_DIM
    nb = rows // lb
    sub = lb // HALF_WINDOW
    n_halo_blocks = rows // HALF_WINDOW
    lo = np.zeros((nb,), np.int32)
    hi = np.zeros((nb,), np.int32)
    for (t_lo, t_hi) in seq_bounds:
        assert t_lo % (dilation * lb) == 0 and t_hi % (dilation * lb) == 0
        lo[t_lo // dilation // lb:t_hi // dilation // lb] = t_lo // dilation
        hi[t_lo // dilation // lb:t_hi // dilation // lb] = t_hi // dilation
    cur = pl.BlockSpec((None, lb, a), lambda r, i, lo, hi: (r, i, 0))
    prev = pl.BlockSpec((None, HALF_WINDOW, a), lambda r, i, lo, hi: (r, jnp.maximum(i * sub - 1, 0), 0))
    nxt = pl.BlockSpec((None, HALF_WINDOW, a),
                       lambda r, i, lo, hi: (r, jnp.minimum((i + 1) * sub, n_halo_blocks - 1), 0))
    bias = _attn_bias(dilation, n_heads, lb)
    return pl.pallas_call(
        functools.partial(_attn_kernel, n_pairs=a // HEAD_PAIR),
        grid_spec=pltpu.PrefetchScalarGridSpec(
            num_scalar_prefetch=2, grid=(dilation, nb),
            in_specs=[cur, prev, cur, nxt, prev, cur, nxt,
                      pl.BlockSpec(bias.shape, lambda r, i, lo, hi: (0, 0, 0))],
            out_specs=[cur, cur]),
        out_shape=[jax.ShapeDtypeStruct((dilation, rows, a), BF16),
                   jax.ShapeDtypeStruct((dilation, rows, a), F32)],
        compiler_params=_cparams(2),
        name=f"attn_d{dilation}",
    )(jnp.asarray(lo), jnp.asarray(hi), q, k, k, k, v, v, v, bias)


def _mix_kernel(seq_ref, first_ref, last_ref, x_ref, up_ref, uc_ref, un_ref, *refs, attn_w, dilations):
    del seq_ref
    nbr = len(dilations)
    branch_refs = refs[:2 * nbr]
    cw_ref, cb_ref, lng_ref, lnb_ref, ga_ref, gc_ref, wo_ref, gt_ref, xo_ref = refs[2 * nbr:2 * nbr + 9]
    bufs = refs[2 * nbr + 9:]
    i = pl.program_id(0)
    tb = x_ref.shape[0]

    def natural(ref, dil, buf):
        if dil == 1:
            return ref[0].astype(F32)
        n_tiles = buf.shape[0]
        for r in range(dil):
            plane = ref[r].astype(F32)
            for c in range(n_tiles):
                buf[c, pl.ds(r, tb // dil, stride=dil), :] = plane[:, LANES * c:LANES * (c + 1)]
        return jnp.concatenate([buf[c] for c in range(n_tiles)], axis=1)

    outs, lses = [], []
    for b, dil in enumerate(dilations):
        outs.append(natural(branch_refs[2 * b], dil, bufs[2 * b]))
        lses.append(natural(branch_refs[2 * b + 1], dil, bufs[2 * b + 1]))
    m = functools.reduce(jnp.maximum, lses)
    es = [jnp.exp(l - m) for l in lses]
    num = functools.reduce(lambda p, q: p + q, [e * o for e, o in zip(es, outs)])
    attn = num * (1.0 / functools.reduce(lambda p, q: p + q, es))
    attn_n = _rms(attn) * ga_ref[...]

    keep_prev = jnp.where(first_ref[i] == 1, 0.0, 1.0)
    keep_next = jnp.where(last_ref[i] == 1, 0.0, 1.0)
    win = jnp.concatenate([up_ref[...].astype(F32) * keep_prev, uc_ref[...].astype(F32),
                           un_ref[...].astype(F32) * keep_next], axis=0)
    n = tb + 2 * CONV_HALO
    acc = None
    for rho in range(8):
        shifted = win if rho == 0 else pltpu.roll(win, n - rho, axis=0)
        for blk in range(2 * CONV_HALO // 8):
            tap = 8 * blk + rho - (CONV_HALO - CONV_WIDTH // 2)
            if 0 <= tap < CONV_WIDTH:
                term = cw_ref[tap:tap + 1, :] * shifted[8 * blk:8 * blk + tb]
                acc = term if acc is None else acc + term
    conv = acc + cb_ref[...]
    mu = jnp.mean(conv, axis=-1, keepdims=True)
    xc = conv - mu
    y = xc * lax.rsqrt(jnp.mean(xc * xc, axis=-1, keepdims=True) + NORM_EPS) * lng_ref[...] + lnb_ref[...]
    y = y * jax.nn.sigmoid(y)
    conv_n = _rms(y) * gc_ref[...]

    out = jnp.dot(attn_n.astype(BF16), wo_ref[:attn_w, :], preferred_element_type=F32)
    out = out + jnp.dot(conv_n.astype(BF16), wo_ref[attn_w:, :], preferred_element_type=F32)
    xo_ref[...] = x_ref[...] + gt_ref[...] * out


def _mix_call(blk_seq, first, last, x, u, branches, mod_l, lw, nsp):
    t, d = x.shape
    c = u.shape[1]
    a = branches[0][0].shape[2]
    tb = MIX_BLOCK
    nb = t // tb
    sub = tb // CONV_HALO
    n_halo_blocks = t // CONV_HALO
    dilations = tuple(o.shape[0] for o, _ in branches)
    row = lambda w: pl.BlockSpec((tb, w), lambda i, *_: (i, 0))
    vec = lambda w: pl.BlockSpec((1, w), lambda i, *_: (0, 0))
    in_specs = [
        row(d),
        pl.BlockSpec((CONV_HALO, c), lambda i, *_: (jnp.maximum(i * sub - 1, 0), 0)),
        row(c),
        pl.BlockSpec((CONV_HALO, c), lambda i, *_: (jnp.minimum((i + 1) * sub, n_halo_blocks - 1), 0)),
    ]
    args = [x, u, u, u]
    for (o, lse), dil in zip(branches, dilations):
        plane = pl.BlockSpec((dil, tb // dil, a), lambda i, *_: (0, i, 0))
        in_specs += [plane, plane]
        args += [o, lse]
    in_specs += [
        pl.BlockSpec((CONV_WIDTH, c), lambda i, *_: (0, 0)),
        vec(c), vec(c), vec(c), vec(a), vec(c),
        pl.BlockSpec((a + c, d), lambda i, *_: (0, 0)),
        pl.BlockSpec((None, 1, d), lambda i, seq, *_: (2 * nsp + seq[i], 0, 0)),
    ]
    args += [lw["conv_w"], lw["conv_b"].reshape(1, c), lw["conv_ln_g"].reshape(1, c),
             lw["conv_ln_b"].reshape(1, c), lw["g_attn_out"].reshape(1, a), lw["g_conv_out"].reshape(1, c),
             lw["w_out_bf"], mod_l]
    return pl.pallas_call(
        functools.partial(_mix_kernel, attn_w=a, dilations=dilations),
        grid_spec=pltpu.PrefetchScalarGridSpec(
            num_scalar_prefetch=3, grid=(nb,), in_specs=in_specs, out_specs=row(d),
            scratch_shapes=[pltpu.VMEM((a // LANES, tb, LANES), F32) for _ in range(2 * len(dilations))]),
        out_shape=jax.ShapeDtypeStruct((t, d), F32),
        compiler_params=_cparams(1),
        name="mix_out",
    )(blk_seq, first, last, *args)


def _router_kernel(seq_ref, x_ref, g_ref, sc_ref, sh_ref, wr_ref, br_ref, h_ref, idx_ref, wt_ref, *, n_experts):
    del seq_ref
    d = x_ref.shape[1]
    h = _rms(x_ref[...]) * g_ref[...]
    h = h * (1.0 + sc_ref[...]) + sh_ref[...]
    _store_tile_major(h_ref, h, d // LANES)
    logits = jnp.dot(h, wr_ref[...], precision=HIGHEST, preferred_element_type=F32) + br_ref[...]
    lt = logits.T[:n_experts]
    eio = lax.broadcasted_iota(jnp.int32, lt.shape, 0)
    vals, idxs = [], []
    for _ in range(TOP_K):
        m = jnp.max(lt, axis=0, keepdims=True)
        ix = jnp.min(jnp.where(lt == m, eio, n_experts), axis=0, keepdims=True)
        vals.append(m)
        idxs.append(ix)
        lt = jnp.where(eio == ix, -jnp.inf, lt)
    es = [jnp.exp(v - vals[0]) for v in vals]
    inv = 1.0 / (es[0] + es[1] + es[2] + es[3])
    for k in range(TOP_K):
        idx_ref[k:k + 1, :] = idxs[k]
        wt_ref[k:k + 1, :] = es[k] * inv


def _router_call(blk_seq, x, mod_l, g_norm, w_router_pad, b_router_pad, n_experts, nsp):
    t, d = x.shape
    tb = TOKEN_BLOCK
    s_tiles = d // LANES
    ep = w_router_pad.shape[1]
    row = pl.BlockSpec((tb, d), lambda i, seq: (i, 0))
    lane_row = pl.BlockSpec((TOP_K, tb), lambda i, seq: (0, i))
    return pl.pallas_call(
        functools.partial(_router_kernel, n_experts=n_experts),
        grid_spec=pltpu.PrefetchScalarGridSpec(
            num_scalar_prefetch=1, grid=(t // tb,),
            in_specs=[row, pl.BlockSpec((1, d), lambda i, seq: (0, 0)), _mod_spec(4, nsp, d), _mod_spec(3, nsp, d),
                      pl.BlockSpec((d, ep), lambda i, seq: (0, 0)), pl.BlockSpec((1, ep), lambda i, seq: (0, 0))],
            out_specs=[pl.BlockSpec((tb * s_tiles, LANES), lambda i, seq: (i, 0)), lane_row, lane_row]),
        out_shape=[jax.ShapeDtypeStruct((t * s_tiles, LANES), F32), jax.ShapeDtypeStruct((TOP_K, t), jnp.int32),
                   jax.ShapeDtypeStruct((TOP_K, t), F32)],
        compiler_params=_cparams(1),
        name="router",
    )(blk_seq, x, g_norm.reshape(1, d), mod_l, mod_l, w_router_pad, b_router_pad)


def _group_by_expert(idx, wt, n_experts, tm):
    k, t = idx.shape
    na = k * t
    p_max = na + n_experts * tm
    nt = p_max // tm
    a_bits = int(np.ceil(np.log2(na)))
    assert n_experts << a_bits < 2 ** 31
    e_flat = idx.reshape(na)
    w_flat = wt.reshape(na)
    order = jnp.sort((e_flat << a_bits) | jnp.arange(na, dtype=jnp.int32)) & ((1 << a_bits) - 1)
    experts = jnp.arange(n_experts, dtype=jnp.int32)
    counts = jnp.sum((e_flat[:, None] == experts[None, :]).astype(jnp.int32), axis=0)
    gstart = jnp.cumsum(counts) - counts
    pcounts = ((counts + tm - 1) // tm) * tm
    pend = jnp.cumsum(pcounts)
    pstart = pend - pcounts
    pp = jnp.arange(p_max, dtype=jnp.int32)
    e_pp = jnp.minimum(jnp.sum((pp[:, None] >= pend[None, :]).astype(jnp.int32), axis=1), n_experts - 1)
    rank = pp - pstart[e_pp]
    valid = rank < counts[e_pp]
    a_id = order[jnp.clip(gstart[e_pp] + rank, 0, na - 1)]
    tok = jnp.where(valid, a_id % t, 0)
    trash = na + jnp.arange(tm, dtype=jnp.int32)
    dest = jnp.where(valid, a_id, na + pp % tm)
    wgt = jnp.where(valid, w_flat[a_id], 0.0)
    n_used = (pend[n_experts - 1] // tm).reshape(1)
    tile_e = e_pp.reshape(nt, tm)[:, 0]
    dest = jnp.concatenate([trash, dest]).reshape(nt + 1, 1, tm)
    return tok.reshape(nt, 1, tm), dest, wgt.reshape(p_max, 1), tile_e, n_used


def _expert_kernel(te_ref, nu_ref, tokc_ref, tokn_ref, dstp_ref, dstc_ref, wgt_ref, h_hbm,
                   wgu_ref, bgu_ref, wd_ref, bd_ref, y_hbm, xbuf, xbs, ybuf, ystage, gsem, ssem, *, s_tiles):
    del te_ref
    j = pl.program_id(0)
    n_used = nu_ref[0]
    tm = xbs.shape[0]
    d_ff = wd_ref.shape[0]
    n_chunks = d_ff // FF_CHUNK
    per_chunk = tm // n_chunks

    def slab(ref, r):
        if isinstance(r, int):
            return ref.at[pl.ds(r * s_tiles, s_tiles)]
        return ref.at[pl.ds(pl.multiple_of(r * s_tiles, s_tiles), s_tiles)]

    def gather_row(tok_ref, r, priority):
        pltpu.make_async_copy(slab(h_hbm, tok_ref[0, r]), slab(xbuf, r), gsem.at[0]).start(priority=priority)

    def scatter_row(dst_ref, src_buf, r, priority):
        pltpu.make_async_copy(slab(src_buf, r), slab(y_hbm, dst_ref[0, r]), ssem.at[0]).start(priority=priority)

    def all_rows(row_fn):
        def body(i, carry):
            row_fn(2 * i, 0)
            row_fn(2 * i + 1, 1)
            return carry
        lax.fori_loop(0, tm // 2, body, 0, unroll=4)

    def wait_rows(buf, sem):
        pltpu.make_async_copy(buf, buf, sem.at[0]).wait()

    @pl.when(j < n_used)
    def _():
        @pl.when(j == 0)
        def _():
            all_rows(lambda r, p: gather_row(tokc_ref, r, p))
            ybuf[...] = jnp.zeros(ybuf.shape, F32)

        wait_rows(xbuf, gsem)
        xbs[...] = _load_tile_major(xbuf, tm, s_tiles).astype(BF16)

        @pl.when(j >= 1)
        def _():
            wait_rows(ystage, ssem)

        ystage[...] = ybuf[...]
        acc = None
        for c in range(n_chunks):
            rows = range(c * per_chunk, (c + 1) * per_chunk)
            gs = slice(c * FF_CHUNK, (c + 1) * FF_CHUNK)
            us = slice(d_ff + c * FF_CHUNK, d_ff + (c + 1) * FF_CHUNK)
            for r in rows:
                gather_row(tokn_ref, r, r % 2)
            gate = jnp.dot(xbs[...], wgu_ref[:, gs], preferred_element_type=F32) + bgu_ref[:, gs]
            for r in rows:
                scatter_row(dstp_ref, ystage, r, r % 2)
            up = jnp.dot(xbs[...], wgu_ref[:, us], preferred_element_type=F32) + bgu_ref[:, us]
            gate = jnp.minimum(gate, SWIGLU_LIMIT)
            up = jnp.clip(up, -SWIGLU_LIMIT, SWIGLU_LIMIT)
            act = (up + 1.0) * gate * jax.nn.sigmoid(SWIGLU_ALPHA * gate)
            part = jnp.dot(act.astype(BF16), wd_ref[gs, :], preferred_element_type=F32)
            acc = part if acc is None else acc + part
        _store_tile_major(ybuf, (acc + bd_ref[...]) * wgt_ref[...], s_tiles)

        @pl.when(j == n_used - 1)
        def _():
            wait_rows(ystage, ssem)
            all_rows(lambda r, p: scatter_row(dstc_ref, ybuf, r, p))
            wait_rows(ybuf, ssem)
            wait_rows(xbuf, gsem)


def _expert_call(h, groups, wgu_bf, bgu, wd_bf, bd, tm):
    tok, dest, wgt, tile_e, n_used = groups
    n_experts, d, two_f = wgu_bf.shape
    s_tiles = d // LANES
    t = h.shape[0] // s_tiles
    d_ff = two_f // 2
    nt = tok.shape[0]
    na = TOP_K * t
    assert tm % (2 * (d_ff // FF_CHUNK)) == 0
    smem_row = lambda fn: pl.BlockSpec((None, 1, tm), fn, memory_space=pltpu.SMEM)
    return pl.pallas_call(
        functools.partial(_expert_kernel, s_tiles=s_tiles),
        grid_spec=pltpu.PrefetchScalarGridSpec(
            num_scalar_prefetch=2, grid=(nt,),
            in_specs=[
                smem_row(lambda j, te, nu: (j, 0, 0)),
                smem_row(lambda j, te, nu: (jnp.minimum(j + 1, nt - 1), 0, 0)),
                smem_row(lambda j, te, nu: (j, 0, 0)),
                smem_row(lambda j, te, nu: (j + 1, 0, 0)),
                pl.BlockSpec((tm, 1), lambda j, te, nu: (j, 0)),
                pl.BlockSpec(memory_space=pl.ANY),
                pl.BlockSpec((None, d, two_f), lambda j, te, nu: (te[j], 0, 0)),
                pl.BlockSpec((None, 1, two_f), lambda j, te, nu: (te[j], 0, 0)),
                pl.BlockSpec((None, d_ff, d), lambda j, te, nu: (te[j], 0, 0)),
                pl.BlockSpec((None, 1, d), lambda j, te, nu: (te[j], 0, 0)),
            ],
            out_specs=pl.BlockSpec(memory_space=pl.ANY),
            scratch_shapes=[pltpu.VMEM((tm * s_tiles, LANES), F32), pltpu.VMEM((tm, d), BF16),
                            pltpu.VMEM((tm * s_tiles, LANES), F32), pltpu.VMEM((tm * s_tiles, LANES), F32),
                            pltpu.SemaphoreType.DMA((1,)), pltpu.SemaphoreType.DMA((1,))]),
        out_shape=jax.ShapeDtypeStruct(((na + tm) * s_tiles, LANES), F32),
        compiler_params=_cparams(1),
        name="expert_ffn",
    )(tile_e, n_used, tok, tok, dest, dest, wgt, h, wgu_bf, bgu.reshape(n_experts, 1, two_f), wd_bf,
      bd.reshape(n_experts, 1, d))


def _final_kernel(seq_ref, x_ref, y0, y1, y2, y3, gt_ref, g_ref, o_ref):
    del seq_ref
    tb, d = x_ref.shape
    s_tiles = d // LANES
    ysum = (_load_tile_major(y0, tb, s_tiles) + _load_tile_major(y1, tb, s_tiles)) + (
        _load_tile_major(y2, tb, s_tiles) + _load_tile_major(y3, tb, s_tiles))
    x = x_ref[...] + gt_ref[...] * ysum
    o_ref[...] = _rms(x) * g_ref[...]


def _final_call(blk_seq, x, y4, mod_prev, g_final, nsp, blk0, nblk):
    t, d = x.shape
    tb = TOKEN_BLOCK
    nb = t // tb
    s_tiles = d // LANES
    in_specs = [pl.BlockSpec((tb, d), lambda i, seq: (blk0 + i, 0))]
    in_specs += [pl.BlockSpec((tb * s_tiles, LANES), lambda i, seq, k=k: (k * nb + blk0 + i, 0))
                 for k in range(TOP_K)]
    in_specs += [pl.BlockSpec((None, 1, d), lambda i, seq: (5 * nsp + seq[blk0 + i], 0, 0)),
                 pl.BlockSpec((1, d), lambda i, seq: (0, 0))]
    return pl.pallas_call(
        _final_kernel,
        grid_spec=pltpu.PrefetchScalarGridSpec(
            num_scalar_prefetch=1, grid=(nblk,), in_specs=in_specs,
            out_specs=pl.BlockSpec((tb, d), lambda i, seq: (i, 0))),
        out_shape=jax.ShapeDtypeStruct((nblk * tb, d), F32),
        compiler_params=_cparams(1),
        name="final_norm",
    )(blk_seq, x, y4, y4, y4, y4, mod_prev, g_final.reshape(1, d))


def _block_tables(seq_lens, tb):
    seq, first, last = [], [], []
    for s, n in enumerate(seq_lens):
        assert n % tb == 0
        nblk = n // tb
        seq += [s] * nblk
        first += [1] + [0] * (nblk - 1)
        last += [0] * (nblk - 1) + [1]
    as_i32 = lambda z: jnp.asarray(np.asarray(z, np.int32))
    return as_i32(seq), as_i32(first), as_i32(last)


def kernel(x_prompt, x_sample, c_prompt, c_sample, w_ada, b_ada, g_norm1, w_in, conv_w, conv_b, conv_ln_g,
           conv_ln_b, g_attn_out, g_conv_out, w_out, g_norm2, w_router, b_router, w_gate_up, b_gate_up,
           w_down, b_down, g_final):
    bp, sp, d = x_prompt.shape
    bs, ss, _ = x_sample.shape
    depth = w_ada.shape[0]
    conv_ch = conv_w.shape[2]
    attn_w = (w_in.shape[2] - 2 * conv_ch) // 3
    n_experts = w_router.shape[2]
    tp, ts = bp * sp, bs * ss
    t = tp + ts
    seq_lens = [sp] * bp + [ss] * bs
    nseq = len(seq_lens)
    nsp = -(-nseq // 8) * 8
    seq_bounds = []
    start = 0
    for n in seq_lens:
        seq_bounds.append((start, start + n))
        start += n

    x = jnp.concatenate([x_prompt.reshape(tp, d), x_sample.reshape(ts, d)], axis=0)
    c_all = jnp.concatenate([c_prompt, c_sample, jnp.zeros((nsp - nseq, d), F32)], axis=0)
    mod = _ada_call(c_all, w_ada, b_ada)
    mod = mod.reshape(depth, nsp, 6, d).transpose(0, 2, 1, 3).reshape(depth, 6 * nsp, 1, d)

    seq_tok, _, _ = _block_tables(seq_lens, TOKEN_BLOCK)
    seq_mix, first_mix, last_mix = _block_tables(seq_lens, MIX_BLOCK)

    ep = -(-n_experts // LANES) * LANES
    w_router_pad = jnp.pad(w_router, ((0, 0), (0, 0), (0, ep - n_experts)))
    b_router_pad = jnp.pad(b_router, ((0, 0), (0, ep - n_experts))).reshape(depth, 1, ep)
    w_in_bf = w_in.astype(BF16)
    w_out_bf = w_out.astype(BF16)
    wgu_bf = w_gate_up.astype(BF16)
    wd_bf = w_down.astype(BF16)

    y4 = None
    for l in range(depth):
        x, qkv, u = _inproj_call(seq_tok, x, y4, mod[l], mod[l - 1] if l else None, g_norm1[l],
                                 w_in_bf[l], attn_w, conv_ch, nsp)
        branches = [_attn_call(q, k, v, seq_bounds) for q, k, v in qkv]
        lw = dict(conv_w=conv_w[l], conv_b=conv_b[l], conv_ln_g=conv_ln_g[l], conv_ln_b=conv_ln_b[l],
                  g_attn_out=g_attn_out[l], g_conv_out=g_conv_out[l], w_out_bf=w_out_bf[l])
        x = _mix_call(seq_mix, first_mix, last_mix, x, u, branches, mod[l], lw, nsp)
        h, idx, wt = _router_call(seq_tok, x, mod[l], g_norm2[l], w_router_pad[l], b_router_pad[l], n_experts, nsp)
        groups = _group_by_expert(idx, wt, n_experts, EXPERT_TILE)
        y4 = _expert_call(h, groups, wgu_bf[l], b_gate_up[l], wd_bf[l], b_down[l], EXPERT_TILE)
    nbp = tp // TOKEN_BLOCK
    out_p = _final_call(seq_tok, x, y4, mod[depth - 1], g_final, nsp, 0, nbp)
    out_s = _final_call(seq_tok, x, y4, mod[depth - 1], g_final, nsp, nbp, ts // TOKEN_BLOCK)
    return out_p.reshape(bp, sp, d), out_s.reshape(bs, ss, d)
```

```python
import functools

import numpy as np
import jax
import jax.numpy as jnp
from jax import lax
from jax.experimental import pallas as pl
from jax.experimental.pallas import tpu as pltpu

F32 = jnp.float32
BF16 = jnp.bfloat16
HIGHEST = lax.Precision.HIGHEST

LANES = 128
HEAD_DIM = 64
HEAD_PAIR = 2 * HEAD_DIM
CONV_WIDTH = 31
CONV_HALO = 16
DILATED_BRANCHES = ((128, 1), (512, 4), (2048, 16))
HALF_WINDOW = 64
TOP_K = 4
SWIGLU_ALPHA = 1.702
SWIGLU_LIMIT = 7.0
NORM_EPS = 1e-6
NEG_BIG = -1e30

TOKEN_BLOCK = 512
MIX_BLOCK = 256
ATTN_BLOCK = 128
EXPERT_TILE = 512
FF_CHUNK = 256
VMEM_LIMIT = 56 * 1024 * 1024


def _cparams(n_axes):
    return pltpu.CompilerParams(dimension_semantics=("arbitrary",) * n_axes, vmem_limit_bytes=VMEM_LIMIT)


def _rms(x):
    return x * lax.rsqrt(jnp.mean(x * x, axis=-1, keepdims=True) + NORM_EPS)


def _load_tile_major(ref, n_rows, s_tiles):
    return jnp.concatenate([ref[pl.ds(s, n_rows, stride=s_tiles), :] for s in range(s_tiles)], axis=1)


def _store_tile_major(ref, val, s_tiles):
    n_rows = val.shape[0]
    for s in range(s_tiles):
        ref[pl.ds(s, n_rows, stride=s_tiles), :] = val[:, LANES * s:LANES * (s + 1)]


def _weighted_expert_sum(y_refs, wcol_ref, shape):
    tb, d = shape
    w = wcol_ref[...]
    terms = [w[:, k:k + 1] * _load_tile_major(y, tb, d // LANES) for k, y in enumerate(y_refs)]
    return (terms[0] + terms[1]) + (terms[2] + terms[3])


def _ada_kernel(c_ref, w_ref, b_ref, o_ref):
    c = c_ref[...]
    s = c * jax.nn.sigmoid(c)
    o_ref[...] = jnp.dot(s, w_ref[...], precision=HIGHEST, preferred_element_type=F32) + b_ref[...]


def _ada_call(c_pad, w_ada, b_ada):
    depth, d, n = w_ada.shape
    nsp = c_pad.shape[0]
    tn = d
    return pl.pallas_call(
        _ada_kernel,
        grid=(depth, n // tn),
        in_specs=[
            pl.BlockSpec((nsp, d), lambda l, j: (0, 0)),
            pl.BlockSpec((None, d, tn), lambda l, j: (l, 0, j)),
            pl.BlockSpec((None, 1, tn), lambda l, j: (l, 0, j)),
        ],
        out_specs=pl.BlockSpec((None, nsp, tn), lambda l, j: (l, 0, j)),
        out_shape=jax.ShapeDtypeStruct((depth, nsp, n), F32),
        compiler_params=_cparams(2),
        name="ada_mod",
    )(c_pad, w_ada, b_ada.reshape(depth, 1, n))


def _inproj_kernel(seq_ref, *refs, attn_w, conv_ch, combine, dilations):
    del seq_ref
    n_reg = 3 * (len(dilations) - 1)
    if combine:
        x_ref, y0, y1, y2, y3, wcol_ref, gt_ref, g_ref, sc_ref, sh_ref, w_ref, xo_ref = refs[:12]
        rest = refs[12:]
        tb = x_ref.shape[0]
        x = x_ref[...] + gt_ref[...] * _weighted_expert_sum((y0, y1, y2, y3), wcol_ref, x_ref.shape)
        xo_ref[...] = x
    else:
        x_ref, g_ref, sc_ref, sh_ref, w_ref = refs[:5]
        rest = refs[5:]
        tb = x_ref.shape[0]
        x = x_ref[...]
    q_ref, k_ref, v_ref = rest[:3]
    regrouped = rest[3:3 + n_reg]
    u_ref, pbuf = rest[3 + n_reg:]
    h = _rms(x) * g_ref[...]
    h = h * (1.0 + sc_ref[...]) + sh_ref[...]
    proj = jnp.dot(h.astype(BF16), w_ref[...], preferred_element_type=F32)
    a = attn_w
    q = proj[:, :a] * (HEAD_DIM ** -0.5)
    q_ref[0] = q.astype(BF16)
    k_ref[0] = proj[:, a:2 * a].astype(BF16)
    v_ref[0] = proj[:, 2 * a:3 * a].astype(BF16)
    cv = proj[:, 3 * a:3 * a + conv_ch]
    cg = proj[:, 3 * a + conv_ch:]
    u_ref[...] = (cv * jax.nn.sigmoid(cg)).astype(BF16)
    tiles_per = a // LANES
    for c in range(3 * tiles_per):
        cols = slice(LANES * c, LANES * (c + 1))
        pbuf[c] = q[:, cols] if c < tiles_per else proj[:, cols]
    for bi, dil in enumerate(dilations[1:]):
        for r in range(dil):
            for c in range(3 * tiles_per):
                dst = regrouped[3 * bi + c // tiles_per]
                cols = slice(LANES * (c % tiles_per), LANES * (c % tiles_per + 1))
                dst[r, :, cols] = pbuf[c, pl.ds(r, tb // dil, stride=dil), :].astype(BF16)


def _mod_spec(comp, nsp, d):
    return pl.BlockSpec((None, 1, d), lambda i, seq: (comp * nsp + seq[i], 0, 0))


def _inproj_call(blk_seq, x, y4, wcol, mod_l, mod_prev, g_norm, w_in_bf, layer, attn_w, conv_ch, nsp):
    t, d = x.shape
    tb = TOKEN_BLOCK
    nb = t // tb
    s_tiles = d // LANES
    combine = y4 is not None
    dilations = tuple(dil for _, dil in DILATED_BRANCHES)
    assert dilations[0] == 1
    row = pl.BlockSpec((tb, d), lambda i, seq: (i, 0))
    in_specs = [row]
    args = [x]
    if combine:
        for k in range(TOP_K):
            in_specs.append(pl.BlockSpec((tb * s_tiles, LANES), lambda i, seq, k=k: (k * nb + i, 0)))
            args.append(y4)
        in_specs += [pl.BlockSpec((tb, LANES), lambda i, seq: (i, 0)), _mod_spec(5, nsp, d)]
        args += [wcol, mod_prev]
    in_specs += [
        pl.BlockSpec((1, d), lambda i, seq: (0, 0)),
        _mod_spec(1, nsp, d),
        _mod_spec(0, nsp, d),
        pl.BlockSpec((None,) + w_in_bf.shape[1:], lambda i, seq: (layer, 0, 0)),
    ]
    args += [g_norm.reshape(1, d), mod_l, mod_l, w_in_bf]
    out_shapes = []
    out_specs = []
    if combine:
        out_shapes.append(jax.ShapeDtypeStruct((t, d), F32))
        out_specs.append(row)
    for dil in dilations:
        for _ in range(3):
            out_shapes.append(jax.ShapeDtypeStruct((dil, t // dil, attn_w), BF16))
            out_specs.append(pl.BlockSpec((dil, tb // dil, attn_w), lambda i, seq: (0, i, 0)))
    out_shapes.append(jax.ShapeDtypeStruct((t, conv_ch), BF16))
    out_specs.append(pl.BlockSpec((tb, conv_ch), lambda i, seq: (i, 0)))
    outs = pl.pallas_call(
        functools.partial(_inproj_kernel, attn_w=attn_w, conv_ch=conv_ch, combine=combine, dilations=dilations),
        grid_spec=pltpu.PrefetchScalarGridSpec(
            num_scalar_prefetch=1, grid=(nb,), in_specs=in_specs, out_specs=out_specs,
            scratch_shapes=[pltpu.VMEM((3 * attn_w // LANES, tb, LANES), F32)]),
        out_shape=out_shapes,
        compiler_params=_cparams(1),
        name="in_proj",
    )(blk_seq, *args)
    if combine:
        x, outs = outs[0], outs[1:]
    qkv = [tuple(outs[3 * b:3 * b + 3]) for b in range(len(dilations))]
    return x, qkv, outs[-1]


def _attn_kernel(lo_ref, hi_ref, q_ref, kp_ref, kc_ref, kn_ref, vp_ref, vc_ref, vn_ref, bias_ref,
                 o_ref, lse_ref, *, n_pairs):
    i = pl.program_id(1)
    lb = q_ref.shape[0]
    kw = lb + 2 * HALF_WINDOW
    kpos = i * lb - HALF_WINDOW + lax.broadcasted_iota(jnp.int32, (1, kw), 1)
    col_ok = (kpos >= lo_ref[i]) & (kpos < hi_ref[i])
    lane = lax.broadcasted_iota(jnp.int32, (1, HEAD_PAIR), 1)
    for j in range(n_pairs):
        sl = slice(HEAD_PAIR * j, HEAD_PAIR * (j + 1))
        qp = q_ref[:, sl]
        kwin = jnp.concatenate([kp_ref[:, sl], kc_ref[:, sl], kn_ref[:, sl]], axis=0)
        vwin = jnp.concatenate([vp_ref[:, sl], vc_ref[:, sl], vn_ref[:, sl]], axis=0)
        o_pair = None
        lse_pair = None
        for b in range(2):
            in_head = (lane >= HEAD_DIM * b) & (lane < HEAD_DIM * (b + 1))
            head_mask = jnp.where(in_head, 1.0, 0.0).astype(BF16)
            s = lax.dot_general(qp * head_mask, kwin, (((1,), (1,)), ((), ())),
                                preferred_element_type=F32)
            s = s + bias_ref[2 * j + b]
            s = jnp.where(col_ok, s, NEG_BIG)
            m = jnp.max(s, axis=-1, keepdims=True)
            p = jnp.exp(s - m)
            den = jnp.sum(p, axis=-1, keepdims=True)
            pv = jnp.dot(p.astype(BF16), vwin, preferred_element_type=F32)
            o_b = pv * (1.0 / den)
            lse_b = jnp.broadcast_to(m + jnp.log(den), (lb, HEAD_PAIR))
            if b == 0:
                o_pair, lse_pair = o_b, lse_b
            else:
                o_pair = jnp.where(in_head, o_b, o_pair)
                lse_pair = jnp.where(in_head, lse_b, lse_pair)
        o_ref[:, sl] = o_pair.astype(BF16)
        lse_ref[:, sl] = lse_pair


def _attn_bias(dilation, n_heads, lb):
    kw = lb + 2 * HALF_WINDOW
    rel = (np.arange(kw)[None, :] - HALF_WINDOW) - np.arange(lb)[:, None]
    slopes = 2.0 ** (-8.0 * np.arange(1, n_heads + 1, dtype=np.float64) / n_heads)
    bias = -slopes[:, None, None] * (dilation * np.abs(rel))[None].astype(np.float64)
    bias = np.where((np.abs(rel) <= HALF_WINDOW)[None], bias, NEG_BIG)
    return jnp.asarray(bias.astype(np.float32))


def _attn_call(q, k, v, seq_bounds):
    dilation, rows, a = q.shape
    lb = ATTN_BLOCK
    n_heads = a // HEAD_DIM
    nb = rows // lb
    sub = lb // HALF_WINDOW
    n_halo_blocks = rows // HALF_WINDOW
    lo = np.zeros((nb,), np.int32)
    hi = np.zeros((nb,), np.int32)
    for (t_lo, t_hi) in seq_bounds:
        assert t_lo % (dilation * lb) == 0 and t_hi % (dilation * lb) == 0
        lo[t_lo // dilation // lb:t_hi // dilation // lb] = t_lo // dilation
        hi[t_lo // dilation // lb:t_hi // dilation // lb] = t_hi // dilation
    cur = pl.BlockSpec((None, lb, a), lambda r, i, lo, hi: (r, i, 0))
    prev = pl.BlockSpec((None, HALF_WINDOW, a), lambda r, i, lo, hi: (r, jnp.maximum(i * sub - 1, 0), 0))
    nxt = pl.BlockSpec((None, HALF_WINDOW, a),
                       lambda r, i, lo, hi: (r, jnp.minimum((i + 1) * sub, n_halo_blocks - 1), 0))
    bias = _attn_bias(dilation, n_heads, lb)
    return pl.pallas_call(
        functools.partial(_attn_kernel, n_pairs=a // HEAD_PAIR),
        grid_spec=pltpu.PrefetchScalarGridSpec(
            num_scalar_prefetch=2, grid=(dilation, nb),
            in_specs=[cur, prev, cur, nxt, prev, cur, nxt,
                      pl.BlockSpec(bias.shape, lambda r, i, lo, hi: (0, 0, 0))],
            out_specs=[cur, cur]),
        out_shape=[jax.ShapeDtypeStruct((dilation, rows, a), BF16),
                   jax.ShapeDtypeStruct((dilation, rows, a), F32)],
        compiler_params=_cparams(2),
        name=f"attn_d{dilation}",
    )(jnp.asarray(lo), jnp.asarray(hi), q, k, k, k, v, v, v, bias)


def _mix_kernel(seq_ref, first_ref, last_ref, x_ref, up_ref, uc_ref, un_ref, *refs, attn_w, dilations):
    del seq_ref
    nbr = len(dilations)
    branch_refs = refs[:2 * nbr]
    cw_ref, cb_ref, lng_ref, lnb_ref, ga_ref, gc_ref, wo_ref, gt_ref, xo_ref = refs[2 * nbr:2 * nbr + 9]
    bufs = refs[2 * nbr + 9:]
    i = pl.program_id(0)
    tb = x_ref.shape[0]

    def natural(ref, dil, buf):
        if dil == 1:
            return ref[0].astype(F32)
        n_tiles = buf.shape[0]
        for r in range(dil):
            plane = ref[r].astype(F32)
            for c in range(n_tiles):
                buf[c, pl.ds(r, tb // dil, stride=dil), :] = plane[:, LANES * c:LANES * (c + 1)]
        return jnp.concatenate([buf[c] for c in range(n_tiles)], axis=1)

    outs, lses = [], []
    for b, dil in enumerate(dilations):
        outs.append(natural(branch_refs[2 * b], dil, bufs[2 * b]))
        lses.append(natural(branch_refs[2 * b + 1], dil, bufs[2 * b + 1]))
    m = functools.reduce(jnp.maximum, lses)
    es = [jnp.exp(l - m) for l in lses]
    num = functools.reduce(lambda p, q: p + q, [e * o for e, o in zip(es, outs)])
    attn = num * (1.0 / functools.reduce(lambda p, q: p + q, es))
    attn_n = _rms(attn) * ga_ref[...]

    keep_prev = jnp.where(first_ref[i] == 1, 0.0, 1.0)
    keep_next = jnp.where(last_ref[i] == 1, 0.0, 1.0)
    win = jnp.concatenate([up_ref[...].astype(F32) * keep_prev, uc_ref[...].astype(F32),
                           un_ref[...].astype(F32) * keep_next], axis=0)
    n = tb + 2 * CONV_HALO
    acc = None
    for rho in range(8):
        shifted = win if rho == 0 else pltpu.roll(win, n - rho, axis=0)
        for blk in range(2 * CONV_HALO // 8):
            tap = 8 * blk + rho - (CONV_HALO - CONV_WIDTH // 2)
            if 0 <= tap < CONV_WIDTH:
                term = cw_ref[tap:tap + 1, :] * shifted[8 * blk:8 * blk + tb]
                acc = term if acc is None else acc + term
    conv = acc + cb_ref[...]
    mu = jnp.mean(conv, axis=-1, keepdims=True)
    xc = conv - mu
    y = xc * lax.rsqrt(jnp.mean(xc * xc, axis=-1, keepdims=True) + NORM_EPS) * lng_ref[...] + lnb_ref[...]
    y = y * jax.nn.sigmoid(y)
    conv_n = _rms(y) * gc_ref[...]

    out = jnp.dot(attn_n.astype(BF16), wo_ref[:attn_w, :], preferred_element_type=F32)
    out = out + jnp.dot(conv_n.astype(BF16), wo_ref[attn_w:, :], preferred_element_type=F32)
    xo_ref[...] = x_ref[...] + gt_ref[...] * out


def _mix_call(blk_seq, first, last, x, u, branches, mod_l, lw, nsp):
    t, d = x.shape
    c = u.shape[1]
    a = branches[0][0].shape[2]
    tb = MIX_BLOCK
    nb = t // tb
    sub = tb // CONV_HALO
    n_halo_blocks = t // CONV_HALO
    dilations = tuple(o.shape[0] for o, _ in branches)
    row = lambda w: pl.BlockSpec((tb, w), lambda i, *_: (i, 0))
    vec = lambda w: pl.BlockSpec((1, w), lambda i, *_: (0, 0))
    in_specs = [
        row(d),
        pl.BlockSpec((CONV_HALO, c), lambda i, *_: (jnp.maximum(i * sub - 1, 0), 0)),
        row(c),
        pl.BlockSpec((CONV_HALO, c), lambda i, *_: (jnp.minimum((i + 1) * sub, n_halo_blocks - 1), 0)),
    ]
    args = [x, u, u, u]
    for (o, lse), dil in zip(branches, dilations):
        plane = pl.BlockSpec((dil, tb // dil, a), lambda i, *_: (0, i, 0))
        in_specs += [plane, plane]
        args += [o, lse]
    in_specs += [
        pl.BlockSpec((CONV_WIDTH, c), lambda i, *_: (0, 0)),
        vec(c), vec(c), vec(c), vec(a), vec(c),
        pl.BlockSpec((None, a + c, d), lambda i, *_: (lw["layer"], 0, 0)),
        pl.BlockSpec((None, 1, d), lambda i, seq, *_: (2 * nsp + seq[i], 0, 0)),
    ]
    args += [lw["conv_w"], lw["conv_b"].reshape(1, c), lw["conv_ln_g"].reshape(1, c),
             lw["conv_ln_b"].reshape(1, c), lw["g_attn_out"].reshape(1, a), lw["g_conv_out"].reshape(1, c),
             lw["w_out_bf"], mod_l]
    return pl.pallas_call(
        functools.partial(_mix_kernel, attn_w=a, dilations=dilations),
        grid_spec=pltpu.PrefetchScalarGridSpec(
            num_scalar_prefetch=3, grid=(nb,), in_specs=in_specs, out_specs=row(d),
            scratch_shapes=[pltpu.VMEM((a // LANES, tb, LANES), F32) for _ in range(2 * len(dilations))]),
        out_shape=jax.ShapeDtypeStruct((t, d), F32),
        compiler_params=_cparams(1),
        name="mix_out",
    )(blk_seq, first, last, *args)


def _router_kernel(seq_ref, x_ref, g_ref, sc_ref, sh_ref, wr_ref, br_ref, h_ref, idx_ref, wcol_ref, *, n_experts):
    del seq_ref
    tb, d = x_ref.shape
    h = _rms(x_ref[...]) * g_ref[...]
    h = h * (1.0 + sc_ref[...]) + sh_ref[...]
    _store_tile_major(h_ref, h, d // LANES)
    logits = jnp.dot(h, wr_ref[...], precision=HIGHEST, preferred_element_type=F32) + br_ref[...]
    lt = logits.T[:n_experts]
    eio = lax.broadcasted_iota(jnp.int32, lt.shape, 0)
    vals, idxs = [], []
    for _ in range(TOP_K):
        m = jnp.max(lt, axis=0, keepdims=True)
        ix = jnp.min(jnp.where(lt == m, eio, n_experts), axis=0, keepdims=True)
        vals.append(m)
        idxs.append(ix)
        lt = jnp.where(eio == ix, -jnp.inf, lt)
    es = [jnp.exp(v - vals[0]) for v in vals]
    inv = 1.0 / (es[0] + es[1] + es[2] + es[3])
    for k in range(TOP_K):
        idx_ref[k:k + 1, :] = idxs[k]
    sub = lax.broadcasted_iota(jnp.int32, (8, tb), 0)
    wrows = jnp.zeros((8, tb), F32)
    for k in range(TOP_K):
        wrows = jnp.where(sub == k, es[k] * inv, wrows)
    wcol_ref[...] = jnp.concatenate([wrows, jnp.zeros((LANES - 8, tb), F32)], axis=0).T


def _router_call(blk_seq, x, mod_l, g_norm, w_router_pad, b_router_pad, layer, n_experts, nsp):
    t, d = x.shape
    tb = TOKEN_BLOCK
    s_tiles = d // LANES
    ep = w_router_pad.shape[2]
    row = pl.BlockSpec((tb, d), lambda i, seq: (i, 0))
    return pl.pallas_call(
        functools.partial(_router_kernel, n_experts=n_experts),
        grid_spec=pltpu.PrefetchScalarGridSpec(
            num_scalar_prefetch=1, grid=(t // tb,),
            in_specs=[row, pl.BlockSpec((1, d), lambda i, seq: (0, 0)), _mod_spec(4, nsp, d), _mod_spec(3, nsp, d),
                      pl.BlockSpec((None, d, ep), lambda i, seq: (layer, 0, 0)),
                      pl.BlockSpec((None, 1, ep), lambda i, seq: (layer, 0, 0))],
            out_specs=[pl.BlockSpec((tb * s_tiles, LANES), lambda i, seq: (i, 0)),
                       pl.BlockSpec((TOP_K, tb), lambda i, seq: (0, i)),
                       pl.BlockSpec((tb, LANES), lambda i, seq: (i, 0))]),
        out_shape=[jax.ShapeDtypeStruct((t * s_tiles, LANES), F32), jax.ShapeDtypeStruct((TOP_K, t), jnp.int32),
                   jax.ShapeDtypeStruct((t, LANES), F32)],
        compiler_params=_cparams(1),
        name="router",
    )(blk_seq, x, g_norm.reshape(1, d), mod_l, mod_l, w_router_pad, b_router_pad)


def _group_by_expert(idx, n_experts, tm):
    k, t = idx.shape
    na = k * t
    nt = na // tm + n_experts
    a_bits = int(np.ceil(np.log2(na)))
    assert n_experts << a_bits < 2 ** 31
    e_flat = idx.reshape(na)
    order = jnp.sort((e_flat << a_bits) | jnp.arange(na, dtype=jnp.int32)) & ((1 << a_bits) - 1)
    experts = jnp.arange(n_experts, dtype=jnp.int32)
    counts = jnp.sum((e_flat[None, :] == experts[:, None]).astype(jnp.int32), axis=1)
    gstart = jnp.cumsum(counts) - counts
    pcounts = ((counts + tm - 1) // tm) * tm
    pend = jnp.cumsum(pcounts)
    pstart = pend - pcounts
    tile_start = jnp.arange(nt, dtype=jnp.int32) * tm
    tile_e = jnp.minimum(jnp.sum((tile_start[:, None] >= pend[None, :]).astype(jnp.int32), axis=1), n_experts - 1)
    rank0 = tile_start - pstart[tile_e]
    n_valid = counts[tile_e] - rank0
    src0 = gstart[tile_e] + rank0
    lane = jnp.arange(tm, dtype=jnp.int32)[None, :]
    valid = lane < n_valid[:, None]
    a_id = order[jnp.clip(src0[:, None] + lane, 0, na - 1)]
    slot = sum((a_id >= s * t).astype(jnp.int32) for s in range(1, k))
    tok = jnp.where(valid, a_id - slot * t, 0)
    dest = jnp.where(valid, a_id, na + lane)
    dest = jnp.concatenate([na + lane, dest], axis=0)
    n_used = (pend[n_experts - 1] // tm).reshape(1)
    return tok.reshape(nt, 1, tm), dest.reshape(nt + 1, 1, tm), tile_e, n_used


def _expert_kernel(te_ref, nu_ref, tokc_ref, tokn_ref, dstp_ref, dstc_ref, h_hbm,
                   wgu_ref, bgu_ref, wd_ref, bd_ref, y_hbm, xbuf, xbs, ybuf, ystage, gsem, ssem, fsem, *, s_tiles):
    del te_ref
    j = pl.program_id(0)
    n_used = nu_ref[0]
    tm = xbs.shape[0]
    d_ff = wd_ref.shape[0]
    n_chunks = d_ff // FF_CHUNK
    assert n_chunks % 2 == 0
    per_chunk = tm // (n_chunks // 2)

    def slab(ref, r):
        if isinstance(r, int):
            return ref.at[pl.ds(r * s_tiles, s_tiles)]
        return ref.at[pl.ds(pl.multiple_of(r * s_tiles, s_tiles), s_tiles)]

    def gather_row(tok_ref, r, priority):
        pltpu.make_async_copy(slab(h_hbm, tok_ref[0, r]), slab(xbuf, r), gsem.at[0]).start(priority=priority)

    def scatter_row(dst_ref, src_buf, r, priority):
        pltpu.make_async_copy(slab(src_buf, r), slab(y_hbm, dst_ref[0, r]), ssem.at[0]).start(priority=priority)

    def all_rows(row_fn):
        def body(i, carry):
            row_fn(2 * i, 0)
            row_fn(2 * i + 1, 1)
            return carry
        lax.fori_loop(0, tm // 2, body, 0, unroll=4)

    def wait_rows(buf, sem):
        pltpu.make_async_copy(buf, buf, sem.at[0]).wait()

    @pl.when(j < n_used)
    def _():
        @pl.when(j == 0)
        def _():
            all_rows(lambda r, p: gather_row(tokc_ref, r, p))
            ybuf[...] = jnp.zeros(ybuf.shape, F32)

        wait_rows(xbuf, gsem)
        xbs[...] = _load_tile_major(xbuf, tm, s_tiles).astype(BF16)

        @pl.when(j >= 1)
        def _():
            wait_rows(ystage, ssem)

        ystage[...] = ybuf[...]

        acc = None
        for c in range(n_chunks):
            gs = slice(c * FF_CHUNK, (c + 1) * FF_CHUNK)
            us = slice(d_ff + c * FF_CHUNK, d_ff + (c + 1) * FF_CHUNK)
            half = n_chunks // 2
            rows = range((c % half) * per_chunk, (c % half + 1) * per_chunk)
            for r in rows:
                if c < half:
                    gather_row(tokn_ref, r, r % 2)
                else:
                    scatter_row(dstp_ref, ystage, r, r % 2)
            gate = jnp.dot(xbs[...], wgu_ref[:, gs], preferred_element_type=F32) + bgu_ref[:, gs]
            up = jnp.dot(xbs[...], wgu_ref[:, us], preferred_element_type=F32) + bgu_ref[:, us]
            gate = jnp.minimum(gate, SWIGLU_LIMIT)
            up = jnp.clip(up, -SWIGLU_LIMIT, SWIGLU_LIMIT)
            act = (up + 1.0) * gate * jax.nn.sigmoid(SWIGLU_ALPHA * gate)
            part = jnp.dot(act.astype(BF16), wd_ref[gs, :], preferred_element_type=F32)
            acc = part if acc is None else acc + part
            pl.semaphore_signal(fsem.at[0], 1)
            pl.semaphore_wait(fsem.at[0], 1)
        _store_tile_major(ybuf, acc + bd_ref[...], s_tiles)

        @pl.when(j == n_used - 1)
        def _():
            wait_rows(ystage, ssem)
            all_rows(lambda r, p: scatter_row(dstc_ref, ybuf, r, p))
            wait_rows(ybuf, ssem)
            wait_rows(xbuf, gsem)


def _expert_call(h, groups, wgu_bf, bgu, wd_bf, bd, layer, tm):
    tok, dest, tile_e, n_used = groups
    _, n_experts, d, two_f = wgu_bf.shape
    depth = wgu_bf.shape[0]
    s_tiles = d // LANES
    t = h.shape[0] // s_tiles
    d_ff = two_f // 2
    nt = tok.shape[0]
    na = TOP_K * t
    assert tm % (2 * (d_ff // FF_CHUNK)) == 0
    smem_row = lambda fn: pl.BlockSpec((None, 1, tm), fn, memory_space=pltpu.SMEM)
    return pl.pallas_call(
        functools.partial(_expert_kernel, s_tiles=s_tiles),
        grid_spec=pltpu.PrefetchScalarGridSpec(
            num_scalar_prefetch=2, grid=(nt,),
            in_specs=[
                smem_row(lambda j, te, nu: (j, 0, 0)),
                smem_row(lambda j, te, nu: (jnp.minimum(j + 1, nt - 1), 0, 0)),
                smem_row(lambda j, te, nu: (j, 0, 0)),
                smem_row(lambda j, te, nu: (j + 1, 0, 0)),
                pl.BlockSpec(memory_space=pl.ANY),
                pl.BlockSpec((None, None, d, two_f), lambda j, te, nu: (layer, te[j], 0, 0)),
                pl.BlockSpec((None, None, 1, two_f), lambda j, te, nu: (layer, te[j], 0, 0)),
                pl.BlockSpec((None, None, d_ff, d), lambda j, te, nu: (layer, te[j], 0, 0)),
                pl.BlockSpec((None, None, 1, d), lambda j, te, nu: (layer, te[j], 0, 0)),
            ],
            out_specs=pl.BlockSpec(memory_space=pl.ANY),
            scratch_shapes=[pltpu.VMEM((tm * s_tiles, LANES), F32), pltpu.VMEM((tm, d), BF16),
                            pltpu.VMEM((tm * s_tiles, LANES), F32), pltpu.VMEM((tm * s_tiles, LANES), F32),
                            pltpu.SemaphoreType.DMA((1,)), pltpu.SemaphoreType.DMA((1,)),
                            pltpu.SemaphoreType.REGULAR((1,))]),
        out_shape=jax.ShapeDtypeStruct(((na + tm) * s_tiles, LANES), F32),
        compiler_params=_cparams(1),
        name="expert_ffn",
    )(tile_e, n_used, tok, tok, dest, dest, h, wgu_bf, bgu.reshape(depth, n_experts, 1, two_f), wd_bf,
      bd.reshape(depth, n_experts, 1, d))


def _final_kernel(seq_ref, x_ref, y0, y1, y2, y3, wcol_ref, gt_ref, g_ref, o_ref):
    del seq_ref
    x = x_ref[...] + gt_ref[...] * _weighted_expert_sum((y0, y1, y2, y3), wcol_ref, x_ref.shape)
    o_ref[...] = _rms(x) * g_ref[...]


def _final_call(blk_seq, x, y4, wcol, mod_prev, g_final, nsp, blk0, nblk):
    t, d = x.shape
    tb = TOKEN_BLOCK
    nb = t // tb
    s_tiles = d // LANES
    in_specs = [pl.BlockSpec((tb, d), lambda i, seq: (blk0 + i, 0))]
    in_specs += [pl.BlockSpec((tb * s_tiles, LANES), lambda i, seq, k=k: (k * nb + blk0 + i, 0))
                 for k in range(TOP_K)]
    in_specs += [pl.BlockSpec((tb, LANES), lambda i, seq: (blk0 + i, 0)),
                 pl.BlockSpec((None, 1, d), lambda i, seq: (5 * nsp + seq[blk0 + i], 0, 0)),
                 pl.BlockSpec((1, d), lambda i, seq: (0, 0))]
    return pl.pallas_call(
        _final_kernel,
        grid_spec=pltpu.PrefetchScalarGridSpec(
            num_scalar_prefetch=1, grid=(nblk,), in_specs=in_specs,
            out_specs=pl.BlockSpec((tb, d), lambda i, seq: (i, 0))),
        out_shape=jax.ShapeDtypeStruct((nblk * tb, d), F32),
        compiler_params=_cparams(1),
        name="final_norm",
    )(blk_seq, x, y4, y4, y4, y4, wcol, mod_prev, g_final.reshape(1, d))


def _block_tables(seq_lens, tb):
    seq, first, last = [], [], []
    for s, n in enumerate(seq_lens):
        assert n % tb == 0
        nblk = n // tb
        seq += [s] * nblk
        first += [1] + [0] * (nblk - 1)
        last += [0] * (nblk - 1) + [1]
    as_i32 = lambda z: jnp.asarray(np.asarray(z, np.int32))
    return as_i32(seq), as_i32(first), as_i32(last)


def kernel(x_prompt, x_sample, c_prompt, c_sample, w_ada, b_ada, g_norm1, w_in, conv_w, conv_b, conv_ln_g,
           conv_ln_b, g_attn_out, g_conv_out, w_out, g_norm2, w_router, b_router, w_gate_up, b_gate_up,
           w_down, b_down, g_final):
    bp, sp, d = x_prompt.shape
    bs, ss, _ = x_sample.shape
    depth = w_ada.shape[0]
    conv_ch = conv_w.shape[2]
    attn_w = (w_in.shape[2] - 2 * conv_ch) // 3
    n_experts = w_router.shape[2]
    tp, ts = bp * sp, bs * ss
    t = tp + ts
    seq_lens = [sp] * bp + [ss] * bs
    nseq = len(seq_lens)
    nsp = -(-nseq // 8) * 8
    seq_bounds = []
    start = 0
    for n in seq_lens:
        seq_bounds.append((start, start + n))
        start += n

    x = jnp.concatenate([x_prompt.reshape(tp, d), x_sample.reshape(ts, d)], axis=0)
    c_all = jnp.concatenate([c_prompt, c_sample, jnp.zeros((nsp - nseq, d), F32)], axis=0)
    mod = _ada_call(c_all, w_ada, b_ada)
    mod = mod.reshape(depth, nsp, 6, d).transpose(0, 2, 1, 3).reshape(depth, 6 * nsp, 1, d)

    seq_tok, _, _ = _block_tables(seq_lens, TOKEN_BLOCK)
    seq_mix, first_mix, last_mix = _block_tables(seq_lens, MIX_BLOCK)

    ep = -(-n_experts // LANES) * LANES
    w_router_pad = jnp.pad(w_router, ((0, 0), (0, 0), (0, ep - n_experts)))
    b_router_pad = jnp.pad(b_router, ((0, 0), (0, ep - n_experts))).reshape(depth, 1, ep)
    w_in_bf = w_in.astype(BF16)
    w_out_bf = w_out.astype(BF16)
    wgu_bf = w_gate_up.astype(BF16)
    wd_bf = w_down.astype(BF16)

    y4 = wcol = None
    for l in range(depth):
        x, qkv, u = _inproj_call(seq_tok, x, y4, wcol, mod[l], mod[l - 1] if l else None, g_norm1[l],
                                 w_in_bf, l, attn_w, conv_ch, nsp)
        branches = [_attn_call(q, k, v, seq_bounds) for q, k, v in qkv]
        lw = dict(conv_w=conv_w[l], conv_b=conv_b[l], conv_ln_g=conv_ln_g[l], conv_ln_b=conv_ln_b[l],
                  g_attn_out=g_attn_out[l], g_conv_out=g_conv_out[l], w_out_bf=w_out_bf, layer=l)
        x = _mix_call(seq_mix, first_mix, last_mix, x, u, branches, mod[l], lw, nsp)
        h, idx, wcol = _router_call(seq_tok, x, mod[l], g_norm2[l], w_router_pad, b_router_pad, l, n_experts, nsp)
        groups = _group_by_expert(idx, n_experts, EXPERT_TILE)
        y4 = _expert_call(h, groups, wgu_bf, b_gate_up, wd_bf, b_down, l, EXPERT_TILE)
    nbp = tp // TOKEN_BLOCK
    out_p = _final_call(seq_tok, x, y4, wcol, mod[depth - 1], g_final, nsp, 0, nbp)
    out_s = _final_call(seq_tok, x, y4, wcol, mod[depth - 1], g_final, nsp, nbp, ts // TOKEN_BLOCK)
    return out_p.reshape(bp, sp, d), out_s.reshape(bs, ss, d)
```

```python
import functools

import numpy as np
import jax
import jax.numpy as jnp
from jax import lax
from jax.experimental import pallas as pl
from jax.experimental.pallas import tpu as pltpu

F32 = jnp.float32
BF16 = jnp.bfloat16
HIGHEST = lax.Precision.HIGHEST

LANES = 128
HEAD_DIM = 64
HEAD_PAIR = 2 * HEAD_DIM
CONV_WIDTH = 31
CONV_HALO = 16
DILATED_BRANCHES = ((128, 1), (512, 4), (2048, 16))
HALF_WINDOW = 64
TOP_K = 4
SWIGLU_ALPHA = 1.702
SWIGLU_LIMIT = 7.0
NORM_EPS = 1e-6
NEG_BIG = -1e30

TOKEN_BLOCK = 512
MIX_BLOCK = 256
ATTN_BLOCK = 512
ATTN_SUB = 128
EXPERT_TILE = 512
FF_CHUNK = 512
VMEM_LIMIT = 56 * 1024 * 1024


def _cparams(n_axes):
    return pltpu.CompilerParams(dimension_semantics=("arbitrary",) * n_axes, vmem_limit_bytes=VMEM_LIMIT)


def _rms(x):
    return x * lax.rsqrt(jnp.mean(x * x, axis=-1, keepdims=True) + NORM_EPS)


def _load_tile_major(ref, n_rows, s_tiles):
    return jnp.concatenate([ref[pl.ds(s, n_rows, stride=s_tiles), :] for s in range(s_tiles)], axis=1)


def _store_tile_major(ref, val, s_tiles):
    n_rows = val.shape[0]
    for s in range(s_tiles):
        ref[pl.ds(s, n_rows, stride=s_tiles), :] = val[:, LANES * s:LANES * (s + 1)]


def _weighted_expert_sum(y_refs, wcol_ref, shape):
    tb, d = shape
    w = wcol_ref[...]
    terms = [w[:, k:k + 1] * _load_tile_major(y, tb, d // LANES) for k, y in enumerate(y_refs)]
    return (terms[0] + terms[1]) + (terms[2] + terms[3])


def _ada_kernel(c_ref, w_ref, b_ref, o_ref):
    c = c_ref[...]
    s = c * jax.nn.sigmoid(c)
    o_ref[...] = jnp.dot(s, w_ref[...], precision=HIGHEST, preferred_element_type=F32) + b_ref[...]


def _ada_call(c_pad, w_ada, b_ada):
    depth, d, n = w_ada.shape
    nsp = c_pad.shape[0]
    tn = d
    return pl.pallas_call(
        _ada_kernel,
        grid=(depth, n // tn),
        in_specs=[
            pl.BlockSpec((nsp, d), lambda l, j: (0, 0)),
            pl.BlockSpec((None, d, tn), lambda l, j: (l, 0, j)),
            pl.BlockSpec((None, 1, tn), lambda l, j: (l, 0, j)),
        ],
        out_specs=pl.BlockSpec((None, nsp, tn), lambda l, j: (l, 0, j)),
        out_shape=jax.ShapeDtypeStruct((depth, nsp, n), F32),
        compiler_params=_cparams(2),
        name="ada_mod",
    )(c_pad, w_ada, b_ada.reshape(depth, 1, n))


def _inproj_kernel(seq_ref, *refs, attn_w, conv_ch, combine, dilations):
    del seq_ref
    n_reg = 3 * (len(dilations) - 1)
    if combine:
        x_ref, y0, y1, y2, y3, wcol_ref, gt_ref, g_ref, sc_ref, sh_ref, w_ref, xo_ref = refs[:12]
        rest = refs[12:]
        tb = x_ref.shape[0]
        x = x_ref[...] + gt_ref[...] * _weighted_expert_sum((y0, y1, y2, y3), wcol_ref, x_ref.shape)
        xo_ref[...] = x
    else:
        x_ref, g_ref, sc_ref, sh_ref, w_ref = refs[:5]
        rest = refs[5:]
        tb = x_ref.shape[0]
        x = x_ref[...]
    q_ref, k_ref, v_ref = rest[:3]
    regrouped = rest[3:3 + n_reg]
    u_ref, pbuf = rest[3 + n_reg:]
    h = _rms(x) * g_ref[...]
    h = h * (1.0 + sc_ref[...]) + sh_ref[...]
    proj = jnp.dot(h.astype(BF16), w_ref[...], preferred_element_type=F32)
    a = attn_w
    q = proj[:, :a] * (HEAD_DIM ** -0.5)
    q_ref[0] = q.astype(BF16)
    k_ref[0] = proj[:, a:2 * a].astype(BF16)
    v_ref[0] = proj[:, 2 * a:3 * a].astype(BF16)
    cv = proj[:, 3 * a:3 * a + conv_ch]
    cg = proj[:, 3 * a + conv_ch:]
    u_ref[...] = (cv * jax.nn.sigmoid(cg)).astype(BF16)
    tiles_per = a // LANES
    for c in range(3 * tiles_per):
        cols = slice(LANES * c, LANES * (c + 1))
        pbuf[c] = q[:, cols] if c < tiles_per else proj[:, cols]
    for bi, dil in enumerate(dilations[1:]):
        for r in range(dil):
            for c in range(3 * tiles_per):
                dst = regrouped[3 * bi + c // tiles_per]
                cols = slice(LANES * (c % tiles_per), LANES * (c % tiles_per + 1))
                dst[r, :, cols] = pbuf[c, pl.ds(r, tb // dil, stride=dil), :].astype(BF16)


def _mod_spec(comp, nsp, d):
    return pl.BlockSpec((None, 1, d), lambda i, seq: (comp * nsp + seq[i], 0, 0))


def _inproj_call(blk_seq, x, y4, wcol, mod_l, mod_prev, g_norm, w_in_bf, layer, attn_w, conv_ch, nsp):
    t, d = x.shape
    tb = TOKEN_BLOCK
    nb = t // tb
    s_tiles = d // LANES
    combine = y4 is not None
    dilations = tuple(dil for _, dil in DILATED_BRANCHES)
    assert dilations[0] == 1
    row = pl.BlockSpec((tb, d), lambda i, seq: (i, 0))
    in_specs = [row]
    args = [x]
    if combine:
        for k in range(TOP_K):
            in_specs.append(pl.BlockSpec((tb * s_tiles, LANES), lambda i, seq, k=k: (k * nb + i, 0)))
            args.append(y4)
        in_specs += [pl.BlockSpec((tb, LANES), lambda i, seq: (i, 0)), _mod_spec(5, nsp, d)]
        args += [wcol, mod_prev]
    in_specs += [
        pl.BlockSpec((1, d), lambda i, seq: (0, 0)),
        _mod_spec(1, nsp, d),
        _mod_spec(0, nsp, d),
        pl.BlockSpec((None,) + w_in_bf.shape[1:], lambda i, seq: (layer, 0, 0)),
    ]
    args += [g_norm.reshape(1, d), mod_l, mod_l, w_in_bf]
    out_shapes = []
    out_specs = []
    if combine:
        out_shapes.append(jax.ShapeDtypeStruct((t, d), F32))
        out_specs.append(row)
    for dil in dilations:
        for _ in range(3):
            out_shapes.append(jax.ShapeDtypeStruct((dil, t // dil, attn_w), BF16))
            out_specs.append(pl.BlockSpec((dil, tb // dil, attn_w), lambda i, seq: (0, i, 0)))
    out_shapes.append(jax.ShapeDtypeStruct((t, conv_ch), BF16))
    out_specs.append(pl.BlockSpec((tb, conv_ch), lambda i, seq: (i, 0)))
    outs = pl.pallas_call(
        functools.partial(_inproj_kernel, attn_w=attn_w, conv_ch=conv_ch, combine=combine, dilations=dilations),
        grid_spec=pltpu.PrefetchScalarGridSpec(
            num_scalar_prefetch=1, grid=(nb,), in_specs=in_specs, out_specs=out_specs,
            scratch_shapes=[pltpu.VMEM((3 * attn_w // LANES, tb, LANES), F32)]),
        out_shape=out_shapes,
        compiler_params=_cparams(1),
        name="in_proj",
    )(blk_seq, *args)
    if combine:
        x, outs = outs[0], outs[1:]
    qkv = [tuple(outs[3 * b:3 * b + 3]) for b in range(len(dilations))]
    return x, qkv, outs[-1]


def _attn_kernel(lo_ref, hi_ref, q_ref, kp_ref, kc_ref, kn_ref, vp_ref, vc_ref, vn_ref, bias_ref,
                 o_ref, lse_ref, *, n_pairs):
    i = pl.program_id(1)
    lb = q_ref.shape[0]
    sb = bias_ref.shape[1]
    kw = sb + 2 * HALF_WINDOW
    lane = lax.broadcasted_iota(jnp.int32, (1, HEAD_PAIR), 1)
    for j in range(n_pairs):
        sl = slice(HEAD_PAIR * j, HEAD_PAIR * (j + 1))
        kfull = jnp.concatenate([kp_ref[:, sl], kc_ref[:, sl], kn_ref[:, sl]], axis=0)
        vfull = jnp.concatenate([vp_ref[:, sl], vc_ref[:, sl], vn_ref[:, sl]], axis=0)
        for t in range(lb // sb):
            rows = slice(t * sb, (t + 1) * sb)
            kpos = i * lb + t * sb - HALF_WINDOW + lax.broadcasted_iota(jnp.int32, (1, kw), 1)
            col_ok = (kpos >= lo_ref[i]) & (kpos < hi_ref[i])
            qp = q_ref[rows, sl]
            kwin = kfull[t * sb:t * sb + kw]
            vwin = vfull[t * sb:t * sb + kw]
            o_pair = None
            lse_pair = None
            for b in range(2):
                in_head = (lane >= HEAD_DIM * b) & (lane < HEAD_DIM * (b + 1))
                head_mask = jnp.where(in_head, 1.0, 0.0).astype(BF16)
                s = lax.dot_general(qp * head_mask, kwin, (((1,), (1,)), ((), ())),
                                    preferred_element_type=F32)
                s = s + bias_ref[2 * j + b]
                s = jnp.where(col_ok, s, NEG_BIG)
                m = jnp.max(s, axis=-1, keepdims=True)
                p = jnp.exp(s - m)
                den = jnp.sum(p, axis=-1, keepdims=True)
                pv = jnp.dot(p.astype(BF16), vwin, preferred_element_type=F32)
                o_b = pv * (1.0 / den)
                lse_b = jnp.broadcast_to(m + jnp.log(den), (sb, HEAD_PAIR))
                if b == 0:
                    o_pair, lse_pair = o_b, lse_b
                else:
                    o_pair = jnp.where(in_head, o_b, o_pair)
                    lse_pair = jnp.where(in_head, lse_b, lse_pair)
            o_ref[rows, sl] = o_pair.astype(BF16)
            lse_ref[rows, sl] = lse_pair


def _attn_bias(dilation, n_heads, lb):
    kw = lb + 2 * HALF_WINDOW
    rel = (np.arange(kw)[None, :] - HALF_WINDOW) - np.arange(lb)[:, None]
    slopes = 2.0 ** (-8.0 * np.arange(1, n_heads + 1, dtype=np.float64) / n_heads)
    bias = -slopes[:, None, None] * (dilation * np.abs(rel))[None].astype(np.float64)
    bias = np.where((np.abs(rel) <= HALF_WINDOW)[None], bias, NEG_BIG)
    return jnp.asarray(bias.astype(np.float32))


def _attn_call(q, k, v, seq_bounds):
    dilation, rows, a = q.shape
    lb = ATTN_BLOCK
    n_heads = a // HEAD_DIM
    nb = rows // lb
    sub = lb // HALF_WINDOW
    n_halo_blocks = rows // HALF_WINDOW
    lo = np.zeros((nb,), np.int32)
    hi = np.zeros((nb,), np.int32)
    for (t_lo, t_hi) in seq_bounds:
        assert t_lo % (dilation * lb) == 0 and t_hi % (dilation * lb) == 0
        lo[t_lo // dilation // lb:t_hi // dilation // lb] = t_lo // dilation
        hi[t_lo // dilation // lb:t_hi // dilation // lb] = t_hi // dilation
    cur = pl.BlockSpec((None, lb, a), lambda r, i, lo, hi: (r, i, 0))
    prev = pl.BlockSpec((None, HALF_WINDOW, a), lambda r, i, lo, hi: (r, jnp.maximum(i * sub - 1, 0), 0))
    nxt = pl.BlockSpec((None, HALF_WINDOW, a),
                       lambda r, i, lo, hi: (r, jnp.minimum((i + 1) * sub, n_halo_blocks - 1), 0))
    bias = _attn_bias(dilation, n_heads, min(ATTN_SUB, lb))
    return pl.pallas_call(
        functools.partial(_attn_kernel, n_pairs=a // HEAD_PAIR),
        grid_spec=pltpu.PrefetchScalarGridSpec(
            num_scalar_prefetch=2, grid=(dilation, nb),
            in_specs=[cur, prev, cur, nxt, prev, cur, nxt,
                      pl.BlockSpec(bias.shape, lambda r, i, lo, hi: (0, 0, 0))],
            out_specs=[cur, cur]),
        out_shape=[jax.ShapeDtypeStruct((dilation, rows, a), BF16),
                   jax.ShapeDtypeStruct((dilation, rows, a), F32)],
        compiler_params=_cparams(2),
        name=f"attn_d{dilation}",
    )(jnp.asarray(lo), jnp.asarray(hi), q, k, k, k, v, v, v, bias)


def _mix_kernel(seq_ref, first_ref, last_ref, x_ref, up_ref, uc_ref, un_ref, *refs, attn_w, dilations):
    del seq_ref
    nbr = len(dilations)
    branch_refs = refs[:2 * nbr]
    cw_ref, cb_ref, lng_ref, lnb_ref, ga_ref, gc_ref, wo_ref, gt_ref, xo_ref = refs[2 * nbr:2 * nbr + 9]
    bufs = refs[2 * nbr + 9:]
    i = pl.program_id(0)
    tb = x_ref.shape[0]

    def natural(ref, dil, buf):
        if dil == 1:
            return ref[0].astype(F32)
        n_tiles = buf.shape[0]
        for r in range(dil):
            plane = ref[r].astype(F32)
            for c in range(n_tiles):
                buf[c, pl.ds(r, tb // dil, stride=dil), :] = plane[:, LANES * c:LANES * (c + 1)]
        return jnp.concatenate([buf[c] for c in range(n_tiles)], axis=1)

    outs, lses = [], []
    for b, dil in enumerate(dilations):
        outs.append(natural(branch_refs[2 * b], dil, bufs[2 * b]))
        lses.append(natural(branch_refs[2 * b + 1], dil, bufs[2 * b + 1]))
    m = functools.reduce(jnp.maximum, lses)
    es = [jnp.exp(l - m) for l in lses]
    num = functools.reduce(lambda p, q: p + q, [e * o for e, o in zip(es, outs)])
    attn = num * (1.0 / functools.reduce(lambda p, q: p + q, es))
    attn_n = _rms(attn) * ga_ref[...]

    keep_prev = jnp.where(first_ref[i] == 1, 0.0, 1.0)
    keep_next = jnp.where(last_ref[i] == 1, 0.0, 1.0)
    win = jnp.concatenate([up_ref[...].astype(F32) * keep_prev, uc_ref[...].astype(F32),
                           un_ref[...].astype(F32) * keep_next], axis=0)
    n = tb + 2 * CONV_HALO
    acc = None
    for rho in range(8):
        shifted = win if rho == 0 else pltpu.roll(win, n - rho, axis=0)
        for blk in range(2 * CONV_HALO // 8):
            tap = 8 * blk + rho - (CONV_HALO - CONV_WIDTH // 2)
            if 0 <= tap < CONV_WIDTH:
                term = cw_ref[tap:tap + 1, :] * shifted[8 * blk:8 * blk + tb]
                acc = term if acc is None else acc + term
    conv = acc + cb_ref[...]
    mu = jnp.mean(conv, axis=-1, keepdims=True)
    xc = conv - mu
    y = xc * lax.rsqrt(jnp.mean(xc * xc, axis=-1, keepdims=True) + NORM_EPS) * lng_ref[...] + lnb_ref[...]
    y = y * jax.nn.sigmoid(y)
    conv_n = _rms(y) * gc_ref[...]

    out = jnp.dot(attn_n.astype(BF16), wo_ref[:attn_w, :], preferred_element_type=F32)
    out = out + jnp.dot(conv_n.astype(BF16), wo_ref[attn_w:, :], preferred_element_type=F32)
    xo_ref[...] = x_ref[...] + gt_ref[...] * out


def _mix_call(blk_seq, first, last, x, u, branches, mod_l, lw, nsp):
    t, d = x.shape
    c = u.shape[1]
    a = branches[0][0].shape[2]
    tb = MIX_BLOCK
    nb = t // tb
    sub = tb // CONV_HALO
    n_halo_blocks = t // CONV_HALO
    dilations = tuple(o.shape[0] for o, _ in branches)
    row = lambda w: pl.BlockSpec((tb, w), lambda i, *_: (i, 0))
    vec = lambda w: pl.BlockSpec((1, w), lambda i, *_: (0, 0))
    in_specs = [
        row(d),
        pl.BlockSpec((CONV_HALO, c), lambda i, *_: (jnp.maximum(i * sub - 1, 0), 0)),
        row(c),
        pl.BlockSpec((CONV_HALO, c), lambda i, *_: (jnp.minimum((i + 1) * sub, n_halo_blocks - 1), 0)),
    ]
    args = [x, u, u, u]
    for (o, lse), dil in zip(branches, dilations):
        plane = pl.BlockSpec((dil, tb // dil, a), lambda i, *_: (0, i, 0))
        in_specs += [plane, plane]
        args += [o, lse]
    in_specs += [
        pl.BlockSpec((CONV_WIDTH, c), lambda i, *_: (0, 0)),
        vec(c), vec(c), vec(c), vec(a), vec(c),
        pl.BlockSpec((None, a + c, d), lambda i, *_: (lw["layer"], 0, 0)),
        pl.BlockSpec((None, 1, d), lambda i, seq, *_: (2 * nsp + seq[i], 0, 0)),
    ]
    args += [lw["conv_w"], lw["conv_b"].reshape(1, c), lw["conv_ln_g"].reshape(1, c),
             lw["conv_ln_b"].reshape(1, c), lw["g_attn_out"].reshape(1, a), lw["g_conv_out"].reshape(1, c),
             lw["w_out_bf"], mod_l]
    return pl.pallas_call(
        functools.partial(_mix_kernel, attn_w=a, dilations=dilations),
        grid_spec=pltpu.PrefetchScalarGridSpec(
            num_scalar_prefetch=3, grid=(nb,), in_specs=in_specs, out_specs=row(d),
            scratch_shapes=[pltpu.VMEM((a // LANES, tb, LANES), F32) for _ in range(2 * len(dilations))]),
        out_shape=jax.ShapeDtypeStruct((t, d), F32),
        compiler_params=_cparams(1),
        name="mix_out",
    )(blk_seq, first, last, *args)


def _router_kernel(seq_ref, x_ref, g_ref, sc_ref, sh_ref, wr_ref, br_ref, h_ref, idx_ref, wcol_ref, *, n_experts):
    del seq_ref
    tb, d = x_ref.shape
    h = _rms(x_ref[...]) * g_ref[...]
    h = h * (1.0 + sc_ref[...]) + sh_ref[...]
    _store_tile_major(h_ref, h, d // LANES)
    w = wr_ref[...]
    h_hi, w_hi = h.astype(BF16), w.astype(BF16)
    h_lo = (h - h_hi.astype(F32)).astype(BF16)
    w_lo = (w - w_hi.astype(F32)).astype(BF16)
    dot = functools.partial(jnp.dot, preferred_element_type=F32)
    logits = dot(h_hi, w_hi) + (dot(h_lo, w_hi) + dot(h_hi, w_lo)) + br_ref[...]
    lt = logits.T[:n_experts]
    eio = lax.broadcasted_iota(jnp.int32, lt.shape, 0)
    vals, idxs = [], []
    for _ in range(TOP_K):
        m = jnp.max(lt, axis=0, keepdims=True)
        ix = jnp.min(jnp.where(lt == m, eio, n_experts), axis=0, keepdims=True)
        vals.append(m)
        idxs.append(ix)
        lt = jnp.where(eio == ix, -jnp.inf, lt)
    es = [jnp.exp(v - vals[0]) for v in vals]
    inv = 1.0 / (es[0] + es[1] + es[2] + es[3])
    for k in range(TOP_K):
        idx_ref[k:k + 1, :] = idxs[k]
    sub = lax.broadcasted_iota(jnp.int32, (8, tb), 0)
    wrows = jnp.zeros((8, tb), F32)
    for k in range(TOP_K):
        wrows = jnp.where(sub == k, es[k] * inv, wrows)
    wcol_ref[...] = jnp.concatenate([wrows, jnp.zeros((LANES - 8, tb), F32)], axis=0).T


def _router_call(blk_seq, x, mod_l, g_norm, w_router_pad, b_router_pad, layer, n_experts, nsp):
    t, d = x.shape
    tb = TOKEN_BLOCK
    s_tiles = d // LANES
    ep = w_router_pad.shape[2]
    row = pl.BlockSpec((tb, d), lambda i, seq: (i, 0))
    return pl.pallas_call(
        functools.partial(_router_kernel, n_experts=n_experts),
        grid_spec=pltpu.PrefetchScalarGridSpec(
            num_scalar_prefetch=1, grid=(t // tb,),
            in_specs=[row, pl.BlockSpec((1, d), lambda i, seq: (0, 0)), _mod_spec(4, nsp, d), _mod_spec(3, nsp, d),
                      pl.BlockSpec((None, d, ep), lambda i, seq: (layer, 0, 0)),
                      pl.BlockSpec((None, 1, ep), lambda i, seq: (layer, 0, 0))],
            out_specs=[pl.BlockSpec((tb * s_tiles, LANES), lambda i, seq: (i, 0)),
                       pl.BlockSpec((TOP_K, tb), lambda i, seq: (0, i)),
                       pl.BlockSpec((tb, LANES), lambda i, seq: (i, 0))]),
        out_shape=[jax.ShapeDtypeStruct((t * s_tiles, LANES), F32), jax.ShapeDtypeStruct((TOP_K, t), jnp.int32),
                   jax.ShapeDtypeStruct((t, LANES), F32)],
        compiler_params=_cparams(1),
        name="router",
    )(blk_seq, x, g_norm.reshape(1, d), mod_l, mod_l, w_router_pad, b_router_pad)


def _group_by_expert(idx, n_experts, tm):
    k, t = idx.shape
    na = k * t
    nt = na // tm + n_experts
    a_bits = int(np.ceil(np.log2(na)))
    assert n_experts << a_bits < 2 ** 31
    e_flat = idx.reshape(na)
    order = jnp.sort((e_flat << a_bits) | jnp.arange(na, dtype=jnp.int32)) & ((1 << a_bits) - 1)
    experts = jnp.arange(n_experts, dtype=jnp.int32)
    counts = jnp.sum((e_flat[None, :] == experts[:, None]).astype(jnp.int32), axis=1)
    gstart = jnp.cumsum(counts) - counts
    pcounts = ((counts + tm - 1) // tm) * tm
    pend = jnp.cumsum(pcounts)
    pstart = pend - pcounts
    tile_start = jnp.arange(nt, dtype=jnp.int32) * tm
    tile_e = jnp.minimum(jnp.sum((tile_start[:, None] >= pend[None, :]).astype(jnp.int32), axis=1), n_experts - 1)
    rank0 = tile_start - pstart[tile_e]
    n_valid = counts[tile_e] - rank0
    src0 = gstart[tile_e] + rank0
    lane = jnp.arange(tm, dtype=jnp.int32)[None, :]
    valid = lane < n_valid[:, None]
    a_id = order[jnp.clip(src0[:, None] + lane, 0, na - 1)]
    slot = sum((a_id >= s * t).astype(jnp.int32) for s in range(1, k))
    tok = jnp.where(valid, a_id - slot * t, 0)
    dest = jnp.where(valid, a_id, na + lane)
    dest = jnp.concatenate([na + lane, dest], axis=0)
    n_used = (pend[n_experts - 1] // tm).reshape(1)
    return tok.reshape(nt, 1, tm), dest.reshape(nt + 1, 1, tm), tile_e, n_used


def _expert_kernel(te_ref, nu_ref, tokc_ref, tokn_ref, dstp_ref, dstc_ref, h_hbm,
                   wgu_ref, bgu_ref, wd_ref, bd_ref, y_hbm, xbuf, xbs, ybuf, ystage, gsem, ssem, fsem, *, s_tiles):
    del te_ref
    j = pl.program_id(0)
    n_used = nu_ref[0]
    tm = xbs.shape[0]
    d_ff = wd_ref.shape[0]
    n_chunks = d_ff // FF_CHUNK
    assert n_chunks % 2 == 0
    per_chunk = tm // (n_chunks // 2)

    def slab(ref, r):
        if isinstance(r, int):
            return ref.at[pl.ds(r * s_tiles, s_tiles)]
        return ref.at[pl.ds(pl.multiple_of(r * s_tiles, s_tiles), s_tiles)]

    def gather_row(tok_ref, r, priority):
        pltpu.make_async_copy(slab(h_hbm, tok_ref[0, r]), slab(xbuf, r), gsem.at[0]).start(priority=priority)

    def scatter_row(dst_ref, src_buf, r, priority):
        pltpu.make_async_copy(slab(src_buf, r), slab(y_hbm, dst_ref[0, r]), ssem.at[0]).start(priority=priority)

    def all_rows(row_fn):
        def body(i, carry):
            row_fn(2 * i, 0)
            row_fn(2 * i + 1, 1)
            return carry
        lax.fori_loop(0, tm // 2, body, 0, unroll=4)

    def wait_rows(buf, sem):
        pltpu.make_async_copy(buf, buf, sem.at[0]).wait()

    @pl.when(j < n_used)
    def _():
        @pl.when(j == 0)
        def _():
            all_rows(lambda r, p: gather_row(tokc_ref, r, p))
            ybuf[...] = jnp.zeros(ybuf.shape, F32)

        wait_rows(xbuf, gsem)
        xbs[...] = _load_tile_major(xbuf, tm, s_tiles).astype(BF16)

        @pl.when(j >= 1)
        def _():
            wait_rows(ystage, ssem)

        ystage[...] = ybuf[...]

        acc = None
        for c in range(n_chunks):
            gs = slice(c * FF_CHUNK, (c + 1) * FF_CHUNK)
            us = slice(d_ff + c * FF_CHUNK, d_ff + (c + 1) * FF_CHUNK)
            half = n_chunks // 2
            rows = range((c % half) * per_chunk, (c % half + 1) * per_chunk)
            for r in rows:
                if c < half:
                    gather_row(tokn_ref, r, r % 2)
                else:
                    scatter_row(dstp_ref, ystage, r, r % 2)
            gate = jnp.dot(xbs[...], wgu_ref[:, gs], preferred_element_type=F32) + bgu_ref[:, gs]
            up = jnp.dot(xbs[...], wgu_ref[:, us], preferred_element_type=F32) + bgu_ref[:, us]
            gate = jnp.minimum(gate, SWIGLU_LIMIT)
            up = jnp.clip(up, -SWIGLU_LIMIT, SWIGLU_LIMIT)
            act = (up + 1.0) * gate * jax.nn.sigmoid(SWIGLU_ALPHA * gate)
            part = jnp.dot(act.astype(BF16), wd_ref[gs, :], preferred_element_type=F32)
            acc = part if acc is None else acc + part
            pl.semaphore_signal(fsem.at[0], 1)
            pl.semaphore_wait(fsem.at[0], 1)
        _store_tile_major(ybuf, acc + bd_ref[...], s_tiles)

        @pl.when(j == n_used - 1)
        def _():
            wait_rows(ystage, ssem)
            all_rows(lambda r, p: scatter_row(dstc_ref, ybuf, r, p))
            wait_rows(ybuf, ssem)
            wait_rows(xbuf, gsem)


def _expert_call(h, groups, wgu_bf, bgu, wd_bf, bd, layer, tm):
    tok, dest, tile_e, n_used = groups
    _, n_experts, d, two_f = wgu_bf.shape
    depth = wgu_bf.shape[0]
    s_tiles = d // LANES
    t = h.shape[0] // s_tiles
    d_ff = two_f // 2
    nt = tok.shape[0]
    na = TOP_K * t
    assert tm % (2 * (d_ff // FF_CHUNK)) == 0
    smem_row = lambda fn: pl.BlockSpec((None, 1, tm), fn, memory_space=pltpu.SMEM)
    return pl.pallas_call(
        functools.partial(_expert_kernel, s_tiles=s_tiles),
        grid_spec=pltpu.PrefetchScalarGridSpec(
            num_scalar_prefetch=2, grid=(nt,),
            in_specs=[
                smem_row(lambda j, te, nu: (j, 0, 0)),
                smem_row(lambda j, te, nu: (jnp.minimum(j + 1, nt - 1), 0, 0)),
                smem_row(lambda j, te, nu: (j, 0, 0)),
                smem_row(lambda j, te, nu: (j + 1, 0, 0)),
                pl.BlockSpec(memory_space=pl.ANY),
                pl.BlockSpec((None, None, d, two_f), lambda j, te, nu: (layer, te[j], 0, 0)),
                pl.BlockSpec((None, None, 1, two_f), lambda j, te, nu: (layer, te[j], 0, 0)),
                pl.BlockSpec((None, None, d_ff, d), lambda j, te, nu: (layer, te[j], 0, 0)),
                pl.BlockSpec((None, None, 1, d), lambda j, te, nu: (layer, te[j], 0, 0)),
            ],
            out_specs=pl.BlockSpec(memory_space=pl.ANY),
            scratch_shapes=[pltpu.VMEM((tm * s_tiles, LANES), F32), pltpu.VMEM((tm, d), BF16),
                            pltpu.VMEM((tm * s_tiles, LANES), F32), pltpu.VMEM((tm * s_tiles, LANES), F32),
                            pltpu.SemaphoreType.DMA((1,)), pltpu.SemaphoreType.DMA((1,)),
                            pltpu.SemaphoreType.REGULAR((1,))]),
        out_shape=jax.ShapeDtypeStruct(((na + tm) * s_tiles, LANES), F32),
        compiler_params=_cparams(1),
        name="expert_ffn",
    )(tile_e, n_used, tok, tok, dest, dest, h, wgu_bf, bgu.reshape(depth, n_experts, 1, two_f), wd_bf,
      bd.reshape(depth, n_experts, 1, d))


def _final_kernel(seq_ref, x_ref, y0, y1, y2, y3, wcol_ref, gt_ref, g_ref, o_ref):
    del seq_ref
    x = x_ref[...] + gt_ref[...] * _weighted_expert_sum((y0, y1, y2, y3), wcol_ref, x_ref.shape)
    o_ref[...] = _rms(x) * g_ref[...]


def _final_call(blk_seq, x, y4, wcol, mod_prev, g_final, nsp, blk0, nblk):
    t, d = x.shape
    tb = TOKEN_BLOCK
    nb = t // tb
    s_tiles = d // LANES
    in_specs = [pl.BlockSpec((tb, d), lambda i, seq: (blk0 + i, 0))]
    in_specs += [pl.BlockSpec((tb * s_tiles, LANES), lambda i, seq, k=k: (k * nb + blk0 + i, 0))
                 for k in range(TOP_K)]
    in_specs += [pl.BlockSpec((tb, LANES), lambda i, seq: (blk0 + i, 0)),
                 pl.BlockSpec((None, 1, d), lambda i, seq: (5 * nsp + seq[blk0 + i], 0, 0)),
                 pl.BlockSpec((1, d), lambda i, seq: (0, 0))]
    return pl.pallas_call(
        _final_kernel,
        grid_spec=pltpu.PrefetchScalarGridSpec(
            num_scalar_prefetch=1, grid=(nblk,), in_specs=in_specs,
            out_specs=pl.BlockSpec((tb, d), lambda i, seq: (i, 0))),
        out_shape=jax.ShapeDtypeStruct((nblk * tb, d), F32),
        compiler_params=_cparams(1),
        name="final_norm",
    )(blk_seq, x, y4, y4, y4, y4, wcol, mod_prev, g_final.reshape(1, d))


def _block_tables(seq_lens, tb):
    seq, first, last = [], [], []
    for s, n in enumerate(seq_lens):
        assert n % tb == 0
        nblk = n // tb
        seq += [s] * nblk
        first += [1] + [0] * (nblk - 1)
        last += [0] * (nblk - 1) + [1]
    as_i32 = lambda z: jnp.asarray(np.asarray(z, np.int32))
    return as_i32(seq), as_i32(first), as_i32(last)


def kernel(x_prompt, x_sample, c_prompt, c_sample, w_ada, b_ada, g_norm1, w_in, conv_w, conv_b, conv_ln_g,
           conv_ln_b, g_attn_out, g_conv_out, w_out, g_norm2, w_router, b_router, w_gate_up, b_gate_up,
           w_down, b_down, g_final):
    bp, sp, d = x_prompt.shape
    bs, ss, _ = x_sample.shape
    depth = w_ada.shape[0]
    conv_ch = conv_w.shape[2]
    attn_w = (w_in.shape[2] - 2 * conv_ch) // 3
    n_experts = w_router.shape[2]
    tp, ts = bp * sp, bs * ss
    t = tp + ts
    seq_lens = [sp] * bp + [ss] * bs
    nseq = len(seq_lens)
    nsp = -(-nseq // 8) * 8
    seq_bounds = []
    start = 0
    for n in seq_lens:
        seq_bounds.append((start, start + n))
        start += n

    x = jnp.concatenate([x_prompt.reshape(tp, d), x_sample.reshape(ts, d)], axis=0)
    c_all = jnp.concatenate([c_prompt, c_sample, jnp.zeros((nsp - nseq, d), F32)], axis=0)
    mod = _ada_call(c_all, w_ada, b_ada)
    mod = mod.reshape(depth, nsp, 6, d).transpose(0, 2, 1, 3).reshape(depth, 6 * nsp, 1, d)

    seq_tok, _, _ = _block_tables(seq_lens, TOKEN_BLOCK)
    seq_mix, first_mix, last_mix = _block_tables(seq_lens, MIX_BLOCK)

    ep = -(-n_experts // LANES) * LANES
    w_router_pad = jnp.pad(w_router, ((0, 0), (0, 0), (0, ep - n_experts)))
    b_router_pad = jnp.pad(b_router, ((0, 0), (0, ep - n_experts))).reshape(depth, 1, ep)
    w_in_bf = w_in.astype(BF16)
    w_out_bf = w_out.astype(BF16)
    wgu_bf = w_gate_up.astype(BF16)
    wd_bf = w_down.astype(BF16)

    y4 = wcol = None
    for l in range(depth):
        x, qkv, u = _inproj_call(seq_tok, x, y4, wcol, mod[l], mod[l - 1] if l else None, g_norm1[l],
                                 w_in_bf, l, attn_w, conv_ch, nsp)
        branches = [_attn_call(q, k, v, seq_bounds) for q, k, v in qkv]
        lw = dict(conv_w=conv_w[l], conv_b=conv_b[l], conv_ln_g=conv_ln_g[l], conv_ln_b=conv_ln_b[l],
                  g_attn_out=g_attn_out[l], g_conv_out=g_conv_out[l], w_out_bf=w_out_bf, layer=l)
        x = _mix_call(seq_mix, first_mix, last_mix, x, u, branches, mod[l], lw, nsp)
        h, idx, wcol = _router_call(seq_tok, x, mod[l], g_norm2[l], w_router_pad, b_router_pad, l, n_experts, nsp)
        groups = _group_by_expert(idx, n_experts, EXPERT_TILE)
        y4 = _expert_call(h, groups, wgu_bf, b_gate_up, wd_bf, b_down, l, EXPERT_TILE)
    nbp = tp // TOKEN_BLOCK
    out_p = _final_call(seq_tok, x, y4, wcol, mod[depth - 1], g_final, nsp, 0, nbp)
    out_s = _final_call(seq_tok, x, y4, wcol, mod[depth - 1], g_final, nsp, nbp, ts // TOKEN_BLOCK)
    return out_p.reshape(bp, sp, d), out_s.reshape(bs, ss, d)
```

```python
import functools

import numpy as np
import jax
import jax.numpy as jnp
from jax import lax
from jax.experimental import pallas as pl
from jax.experimental.pallas import tpu as pltpu

F32 = jnp.float32
BF16 = jnp.bfloat16
HIGHEST = lax.Precision.HIGHEST

LANES = 128
HEAD_DIM = 64
HEAD_PAIR = 2 * HEAD_DIM
CONV_WIDTH = 31
CONV_HALO = 16
DILATED_BRANCHES = ((128, 1), (512, 4), (2048, 16))
HALF_WINDOW = 64
TOP_K = 4
SWIGLU_ALPHA = 1.702
SWIGLU_LIMIT = 7.0
NORM_EPS = 1e-6
NEG_BIG = -1e30

TOKEN_BLOCK = 512
MIX_BLOCK = 512
ATTN_BLOCK = 512
ATTN_SUB = 128
EXPERT_TILE = 512
FF_CHUNK = 512
VMEM_LIMIT = 56 * 1024 * 1024


def _cparams(n_axes):
    return pltpu.CompilerParams(dimension_semantics=("arbitrary",) * n_axes, vmem_limit_bytes=VMEM_LIMIT)


def _rms(x):
    return x * lax.rsqrt(jnp.mean(x * x, axis=-1, keepdims=True) + NORM_EPS)


def _load_tile_major(ref, n_rows, s_tiles):
    return jnp.concatenate([ref[pl.ds(s, n_rows, stride=s_tiles), :] for s in range(s_tiles)], axis=1)


def _store_tile_major(ref, val, s_tiles):
    n_rows = val.shape[0]
    for s in range(s_tiles):
        ref[pl.ds(s, n_rows, stride=s_tiles), :] = val[:, LANES * s:LANES * (s + 1)]


def _weighted_expert_sum(y_refs, wcol_ref, shape):
    tb, d = shape
    w = wcol_ref[...]
    terms = [w[:, k:k + 1] * _load_tile_major(y, tb, d // LANES) for k, y in enumerate(y_refs)]
    return (terms[0] + terms[1]) + (terms[2] + terms[3])


def _ada_kernel(c_ref, w_ref, b_ref, o_ref):
    c = c_ref[...]
    s = c * jax.nn.sigmoid(c)
    o_ref[...] = jnp.dot(s, w_ref[...], precision=HIGHEST, preferred_element_type=F32) + b_ref[...]


def _ada_call(c_pad, w_ada, b_ada):
    depth, d, n = w_ada.shape
    nsp = c_pad.shape[0]
    tn = d
    return pl.pallas_call(
        _ada_kernel,
        grid=(depth, n // tn),
        in_specs=[
            pl.BlockSpec((nsp, d), lambda l, j: (0, 0)),
            pl.BlockSpec((None, d, tn), lambda l, j: (l, 0, j)),
            pl.BlockSpec((None, 1, tn), lambda l, j: (l, 0, j)),
        ],
        out_specs=pl.BlockSpec((None, nsp, tn), lambda l, j: (l, 0, j)),
        out_shape=jax.ShapeDtypeStruct((depth, nsp, n), F32),
        compiler_params=_cparams(2),
        name="ada_mod",
    )(c_pad, w_ada, b_ada.reshape(depth, 1, n))


def _inproj_kernel(seq_ref, *refs, attn_w, conv_ch, combine, dilations):
    del seq_ref
    n_reg = 3 * (len(dilations) - 1)
    if combine:
        x_ref, y0, y1, y2, y3, wcol_ref, gt_ref, g_ref, sc_ref, sh_ref, w_ref, xo_ref = refs[:12]
        rest = refs[12:]
        tb = x_ref.shape[0]
        x = x_ref[...] + gt_ref[...] * _weighted_expert_sum((y0, y1, y2, y3), wcol_ref, x_ref.shape)
        xo_ref[...] = x
    else:
        x_ref, g_ref, sc_ref, sh_ref, w_ref = refs[:5]
        rest = refs[5:]
        tb = x_ref.shape[0]
        x = x_ref[...]
    q_ref, k_ref, v_ref = rest[:3]
    regrouped = rest[3:3 + n_reg]
    u_ref, pbuf = rest[3 + n_reg:]
    h = _rms(x) * g_ref[...]
    h = h * (1.0 + sc_ref[...]) + sh_ref[...]
    proj = jnp.dot(h.astype(BF16), w_ref[...], preferred_element_type=F32)
    a = attn_w
    q = proj[:, :a] * (HEAD_DIM ** -0.5)
    q_ref[0] = q.astype(BF16)
    k_ref[0] = proj[:, a:2 * a].astype(BF16)
    v_ref[0] = proj[:, 2 * a:3 * a].astype(BF16)
    cv = proj[:, 3 * a:3 * a + conv_ch]
    cg = proj[:, 3 * a + conv_ch:]
    u_ref[...] = (cv * jax.nn.sigmoid(cg)).astype(BF16)
    tiles_per = a // LANES
    for c in range(3 * tiles_per):
        cols = slice(LANES * c, LANES * (c + 1))
        pbuf[c] = q[:, cols] if c < tiles_per else proj[:, cols]
    for bi, dil in enumerate(dilations[1:]):
        for r in range(dil):
            for c in range(3 * tiles_per):
                dst = regrouped[3 * bi + c // tiles_per]
                cols = slice(LANES * (c % tiles_per), LANES * (c % tiles_per + 1))
                dst[r, :, cols] = pbuf[c, pl.ds(r, tb // dil, stride=dil), :].astype(BF16)


def _mod_spec(comp, nsp, d):
    return pl.BlockSpec((None, 1, d), lambda i, seq: (comp * nsp + seq[i], 0, 0))


def _inproj_call(blk_seq, x, y4, wcol, mod_l, mod_prev, g_norm, w_in_bf, layer, attn_w, conv_ch, nsp):
    t, d = x.shape
    tb = TOKEN_BLOCK
    nb = t // tb
    s_tiles = d // LANES
    combine = y4 is not None
    dilations = tuple(dil for _, dil in DILATED_BRANCHES)
    assert dilations[0] == 1
    row = pl.BlockSpec((tb, d), lambda i, seq: (i, 0))
    in_specs = [row]
    args = [x]
    if combine:
        for k in range(TOP_K):
            in_specs.append(pl.BlockSpec((tb * s_tiles, LANES), lambda i, seq, k=k: (k * nb + i, 0)))
            args.append(y4)
        in_specs += [pl.BlockSpec((tb, LANES), lambda i, seq: (i, 0)), _mod_spec(5, nsp, d)]
        args += [wcol, mod_prev]
    in_specs += [
        pl.BlockSpec((1, d), lambda i, seq: (0, 0)),
        _mod_spec(1, nsp, d),
        _mod_spec(0, nsp, d),
        pl.BlockSpec((None,) + w_in_bf.shape[1:], lambda i, seq: (layer, 0, 0)),
    ]
    args += [g_norm.reshape(1, d), mod_l, mod_l, w_in_bf]
    out_shapes = []
    out_specs = []
    if combine:
        out_shapes.append(jax.ShapeDtypeStruct((t, d), F32))
        out_specs.append(row)
    for dil in dilations:
        for _ in range(3):
            out_shapes.append(jax.ShapeDtypeStruct((dil, t // dil, attn_w), BF16))
            out_specs.append(pl.BlockSpec((dil, tb // dil, attn_w), lambda i, seq: (0, i, 0)))
    out_shapes.append(jax.ShapeDtypeStruct((t, conv_ch), BF16))
    out_specs.append(pl.BlockSpec((tb, conv_ch), lambda i, seq: (i, 0)))
    outs = pl.pallas_call(
        functools.partial(_inproj_kernel, attn_w=attn_w, conv_ch=conv_ch, combine=combine, dilations=dilations),
        grid_spec=pltpu.PrefetchScalarGridSpec(
            num_scalar_prefetch=1, grid=(nb,), in_specs=in_specs, out_specs=out_specs,
            scratch_shapes=[pltpu.VMEM((3 * attn_w // LANES, tb, LANES), F32)]),
        out_shape=out_shapes,
        compiler_params=_cparams(1),
        name="in_proj",
    )(blk_seq, *args)
    if combine:
        x, outs = outs[0], outs[1:]
    qkv = [tuple(outs[3 * b:3 * b + 3]) for b in range(len(dilations))]
    return x, qkv, outs[-1]


def _attn_kernel(lo_ref, hi_ref, q_ref, kp_ref, kc_ref, kn_ref, vp_ref, vc_ref, vn_ref, bias_ref,
                 o_ref, lse_ref, *, n_pairs):
    i = pl.program_id(1)
    lb = q_ref.shape[0]
    sb = bias_ref.shape[1]
    kw = sb + 2 * HALF_WINDOW
    lane = lax.broadcasted_iota(jnp.int32, (1, HEAD_PAIR), 1)
    for j in range(n_pairs):
        sl = slice(HEAD_PAIR * j, HEAD_PAIR * (j + 1))
        kfull = jnp.concatenate([kp_ref[:, sl], kc_ref[:, sl], kn_ref[:, sl]], axis=0)
        vfull = jnp.concatenate([vp_ref[:, sl], vc_ref[:, sl], vn_ref[:, sl]], axis=0)
        for t in range(lb // sb):
            rows = slice(t * sb, (t + 1) * sb)
            kpos = i * lb + t * sb - HALF_WINDOW + lax.broadcasted_iota(jnp.int32, (1, kw), 1)
            col_ok = (kpos >= lo_ref[i]) & (kpos < hi_ref[i])
            qp = q_ref[rows, sl]
            kwin = kfull[t * sb:t * sb + kw]
            vwin = vfull[t * sb:t * sb + kw]
            o_pair = None
            lse_pair = None
            for b in range(2):
                in_head = (lane >= HEAD_DIM * b) & (lane < HEAD_DIM * (b + 1))
                head_mask = jnp.where(in_head, 1.0, 0.0).astype(BF16)
                s = lax.dot_general(qp * head_mask, kwin, (((1,), (1,)), ((), ())),
                                    preferred_element_type=F32)
                s = s + bias_ref[2 * j + b]
                s = jnp.where(col_ok, s, NEG_BIG)
                m = jnp.max(s, axis=-1, keepdims=True)
                p = jnp.exp(s - m)
                den = jnp.sum(p, axis=-1, keepdims=True)
                pv = jnp.dot(p.astype(BF16), vwin, preferred_element_type=F32)
                o_b = pv * (1.0 / den)
                lse_b = jnp.broadcast_to(m + jnp.log(den), (sb, HEAD_PAIR))
                if b == 0:
                    o_pair, lse_pair = o_b, lse_b
                else:
                    o_pair = jnp.where(in_head, o_b, o_pair)
                    lse_pair = jnp.where(in_head, lse_b, lse_pair)
            o_ref[rows, sl] = o_pair.astype(BF16)
            lse_ref[rows, sl] = lse_pair


def _attn_bias(dilation, n_heads, lb):
    kw = lb + 2 * HALF_WINDOW
    rel = (np.arange(kw)[None, :] - HALF_WINDOW) - np.arange(lb)[:, None]
    slopes = 2.0 ** (-8.0 * np.arange(1, n_heads + 1, dtype=np.float64) / n_heads)
    bias = -slopes[:, None, None] * (dilation * np.abs(rel))[None].astype(np.float64)
    bias = np.where((np.abs(rel) <= HALF_WINDOW)[None], bias, NEG_BIG)
    return jnp.asarray(bias.astype(np.float32))


def _attn_call(q, k, v, seq_bounds):
    dilation, rows, a = q.shape
    lb = ATTN_BLOCK
    n_heads = a // HEAD_DIM
    nb = rows // lb
    sub = lb // HALF_WINDOW
    n_halo_blocks = rows // HALF_WINDOW
    lo = np.zeros((nb,), np.int32)
    hi = np.zeros((nb,), np.int32)
    for (t_lo, t_hi) in seq_bounds:
        assert t_lo % (dilation * lb) == 0 and t_hi % (dilation * lb) == 0
        lo[t_lo // dilation // lb:t_hi // dilation // lb] = t_lo // dilation
        hi[t_lo // dilation // lb:t_hi // dilation // lb] = t_hi // dilation
    cur = pl.BlockSpec((None, lb, a), lambda r, i, lo, hi: (r, i, 0))
    prev = pl.BlockSpec((None, HALF_WINDOW, a), lambda r, i, lo, hi: (r, jnp.maximum(i * sub - 1, 0), 0))
    nxt = pl.BlockSpec((None, HALF_WINDOW, a),
                       lambda r, i, lo, hi: (r, jnp.minimum((i + 1) * sub, n_halo_blocks - 1), 0))
    bias = _attn_bias(dilation, n_heads, min(ATTN_SUB, lb))
    return pl.pallas_call(
        functools.partial(_attn_kernel, n_pairs=a // HEAD_PAIR),
        grid_spec=pltpu.PrefetchScalarGridSpec(
            num_scalar_prefetch=2, grid=(dilation, nb),
            in_specs=[cur, prev, cur, nxt, prev, cur, nxt,
                      pl.BlockSpec(bias.shape, lambda r, i, lo, hi: (0, 0, 0))],
            out_specs=[cur, cur]),
        out_shape=[jax.ShapeDtypeStruct((dilation, rows, a), BF16),
                   jax.ShapeDtypeStruct((dilation, rows, a), F32)],
        compiler_params=_cparams(2),
        name=f"attn_d{dilation}",
    )(jnp.asarray(lo), jnp.asarray(hi), q, k, k, k, v, v, v, bias)


def _mix_kernel(seq_ref, first_ref, last_ref, x_ref, up_ref, uc_ref, un_ref, *refs, attn_w, dilations):
    del seq_ref
    nbr = len(dilations)
    branch_refs = refs[:2 * nbr]
    cw_ref, cb_ref, lng_ref, lnb_ref, ga_ref, gc_ref, wo_ref, gt_ref, xo_ref = refs[2 * nbr:2 * nbr + 9]
    bufs = refs[2 * nbr + 9:]
    i = pl.program_id(0)
    tb = x_ref.shape[0]

    def natural(ref, dil, buf):
        if dil == 1:
            return ref[0].astype(F32)
        n_tiles = buf.shape[0]
        for r in range(dil):
            plane = ref[r].astype(F32)
            for c in range(n_tiles):
                buf[c, pl.ds(r, tb // dil, stride=dil), :] = plane[:, LANES * c:LANES * (c + 1)]
        return jnp.concatenate([buf[c] for c in range(n_tiles)], axis=1)

    outs, lses = [], []
    for b, dil in enumerate(dilations):
        outs.append(natural(branch_refs[2 * b], dil, bufs[2 * b]))
        lses.append(natural(branch_refs[2 * b + 1], dil, bufs[2 * b + 1]))
    m = functools.reduce(jnp.maximum, lses)
    es = [jnp.exp(l - m) for l in lses]
    num = functools.reduce(lambda p, q: p + q, [e * o for e, o in zip(es, outs)])
    attn = num * (1.0 / functools.reduce(lambda p, q: p + q, es))
    attn_n = _rms(attn) * ga_ref[...]

    keep_prev = jnp.where(first_ref[i] == 1, 0.0, 1.0)
    keep_next = jnp.where(last_ref[i] == 1, 0.0, 1.0)
    win = jnp.concatenate([up_ref[...].astype(F32) * keep_prev, uc_ref[...].astype(F32),
                           un_ref[...].astype(F32) * keep_next], axis=0)
    n = tb + 2 * CONV_HALO
    acc = None
    for rho in range(8):
        shifted = win if rho == 0 else pltpu.roll(win, n - rho, axis=0)
        for blk in range(2 * CONV_HALO // 8):
            tap = 8 * blk + rho - (CONV_HALO - CONV_WIDTH // 2)
            if 0 <= tap < CONV_WIDTH:
                term = cw_ref[tap:tap + 1, :] * shifted[8 * blk:8 * blk + tb]
                acc = term if acc is None else acc + term
    conv = acc + cb_ref[...]
    mu = jnp.mean(conv, axis=-1, keepdims=True)
    xc = conv - mu
    y = xc * lax.rsqrt(jnp.mean(xc * xc, axis=-1, keepdims=True) + NORM_EPS) * lng_ref[...] + lnb_ref[...]
    y = y * jax.nn.sigmoid(y)
    conv_n = _rms(y) * gc_ref[...]

    out = jnp.dot(attn_n.astype(BF16), wo_ref[:attn_w, :], preferred_element_type=F32)
    out = out + jnp.dot(conv_n.astype(BF16), wo_ref[attn_w:, :], preferred_element_type=F32)
    xo_ref[...] = x_ref[...] + gt_ref[...] * out


def _mix_call(blk_seq, first, last, x, u, branches, mod_l, lw, nsp):
    t, d = x.shape
    c = u.shape[1]
    a = branches[0][0].shape[2]
    tb = MIX_BLOCK
    nb = t // tb
    sub = tb // CONV_HALO
    n_halo_blocks = t // CONV_HALO
    dilations = tuple(o.shape[0] for o, _ in branches)
    row = lambda w: pl.BlockSpec((tb, w), lambda i, *_: (i, 0))
    vec = lambda w: pl.BlockSpec((1, w), lambda i, *_: (0, 0))
    in_specs = [
        row(d),
        pl.BlockSpec((CONV_HALO, c), lambda i, *_: (jnp.maximum(i * sub - 1, 0), 0)),
        row(c),
        pl.BlockSpec((CONV_HALO, c), lambda i, *_: (jnp.minimum((i + 1) * sub, n_halo_blocks - 1), 0)),
    ]
    args = [x, u, u, u]
    for (o, lse), dil in zip(branches, dilations):
        plane = pl.BlockSpec((dil, tb // dil, a), lambda i, *_: (0, i, 0))
        in_specs += [plane, plane]
        args += [o, lse]
    in_specs += [
        pl.BlockSpec((CONV_WIDTH, c), lambda i, *_: (0, 0)),
        vec(c), vec(c), vec(c), vec(a), vec(c),
        pl.BlockSpec((None, a + c, d), lambda i, *_: (lw["layer"], 0, 0)),
        pl.BlockSpec((None, 1, d), lambda i, seq, *_: (2 * nsp + seq[i], 0, 0)),
    ]
    args += [lw["conv_w"], lw["conv_b"].reshape(1, c), lw["conv_ln_g"].reshape(1, c),
             lw["conv_ln_b"].reshape(1, c), lw["g_attn_out"].reshape(1, a), lw["g_conv_out"].reshape(1, c),
             lw["w_out_bf"], mod_l]
    return pl.pallas_call(
        functools.partial(_mix_kernel, attn_w=a, dilations=dilations),
        grid_spec=pltpu.PrefetchScalarGridSpec(
            num_scalar_prefetch=3, grid=(nb,), in_specs=in_specs, out_specs=row(d),
            scratch_shapes=[pltpu.VMEM((a // LANES, tb, LANES), F32) for _ in range(2 * len(dilations))]),
        out_shape=jax.ShapeDtypeStruct((t, d), F32),
        compiler_params=_cparams(1),
        name="mix_out",
    )(blk_seq, first, last, *args)


def _router_kernel(seq_ref, x_ref, g_ref, sc_ref, sh_ref, wr_ref, br_ref, h_ref, idx_ref, wcol_ref, *, n_experts):
    del seq_ref
    tb, d = x_ref.shape
    h = _rms(x_ref[...]) * g_ref[...]
    h = h * (1.0 + sc_ref[...]) + sh_ref[...]
    _store_tile_major(h_ref, h, d // LANES)
    w = wr_ref[...]
    h_hi, w_hi = h.astype(BF16), w.astype(BF16)
    h_lo = (h - h_hi.astype(F32)).astype(BF16)
    w_lo = (w - w_hi.astype(F32)).astype(BF16)
    dot = functools.partial(jnp.dot, preferred_element_type=F32)
    logits = dot(h_hi, w_hi) + (dot(h_lo, w_hi) + dot(h_hi, w_lo)) + br_ref[...]
    lt = logits.T[:n_experts]
    eio = lax.broadcasted_iota(jnp.int32, lt.shape, 0)
    vals, idxs = [], []
    for _ in range(TOP_K):
        m = jnp.max(lt, axis=0, keepdims=True)
        ix = jnp.min(jnp.where(lt == m, eio, n_experts), axis=0, keepdims=True)
        vals.append(m)
        idxs.append(ix)
        lt = jnp.where(eio == ix, -jnp.inf, lt)
    es = [jnp.exp(v - vals[0]) for v in vals]
    inv = 1.0 / (es[0] + es[1] + es[2] + es[3])
    for k in range(TOP_K):
        idx_ref[k:k + 1, :] = idxs[k]
    sub = lax.broadcasted_iota(jnp.int32, (8, tb), 0)
    wrows = jnp.zeros((8, tb), F32)
    for k in range(TOP_K):
        wrows = jnp.where(sub == k, es[k] * inv, wrows)
    wcol_ref[...] = jnp.concatenate([wrows, jnp.zeros((LANES - 8, tb), F32)], axis=0).T


def _router_call(blk_seq, x, mod_l, g_norm, w_router_pad, b_router_pad, layer, n_experts, nsp):
    t, d = x.shape
    tb = TOKEN_BLOCK
    s_tiles = d // LANES
    ep = w_router_pad.shape[2]
    row = pl.BlockSpec((tb, d), lambda i, seq: (i, 0))
    return pl.pallas_call(
        functools.partial(_router_kernel, n_experts=n_experts),
        grid_spec=pltpu.PrefetchScalarGridSpec(
            num_scalar_prefetch=1, grid=(t // tb,),
            in_specs=[row, pl.BlockSpec((1, d), lambda i, seq: (0, 0)), _mod_spec(4, nsp, d), _mod_spec(3, nsp, d),
                      pl.BlockSpec((None, d, ep), lambda i, seq: (layer, 0, 0)),
                      pl.BlockSpec((None, 1, ep), lambda i, seq: (layer, 0, 0))],
            out_specs=[pl.BlockSpec((tb * s_tiles, LANES), lambda i, seq: (i, 0)),
                       pl.BlockSpec((TOP_K, tb), lambda i, seq: (0, i)),
                       pl.BlockSpec((tb, LANES), lambda i, seq: (i, 0))]),
        out_shape=[jax.ShapeDtypeStruct((t * s_tiles, LANES), F32), jax.ShapeDtypeStruct((TOP_K, t), jnp.int32),
                   jax.ShapeDtypeStruct((t, LANES), F32)],
        compiler_params=_cparams(1),
        name="router",
    )(blk_seq, x, g_norm.reshape(1, d), mod_l, mod_l, w_router_pad, b_router_pad)


def _group_by_expert(idx, n_experts, tm):
    k, t = idx.shape
    na = k * t
    nt = na // tm + n_experts
    a_bits = int(np.ceil(np.log2(na)))
    assert n_experts << a_bits < 2 ** 31
    e_flat = idx.reshape(na)
    order = jnp.sort((e_flat << a_bits) | jnp.arange(na, dtype=jnp.int32)) & ((1 << a_bits) - 1)
    experts = jnp.arange(n_experts, dtype=jnp.int32)
    counts = jnp.sum((e_flat[None, :] == experts[:, None]).astype(jnp.int32), axis=1)
    gstart = jnp.cumsum(counts) - counts
    pcounts = ((counts + tm - 1) // tm) * tm
    pend = jnp.cumsum(pcounts)
    pstart = pend - pcounts
    tile_start = jnp.arange(nt, dtype=jnp.int32) * tm
    tile_e = jnp.minimum(jnp.sum((tile_start[:, None] >= pend[None, :]).astype(jnp.int32), axis=1), n_experts - 1)
    rank0 = tile_start - pstart[tile_e]
    n_valid = counts[tile_e] - rank0
    src0 = gstart[tile_e] + rank0
    lane = jnp.arange(tm, dtype=jnp.int32)[None, :]
    valid = lane < n_valid[:, None]
    a_id = order[jnp.clip(src0[:, None] + lane, 0, na - 1)]
    slot = sum((a_id >= s * t).astype(jnp.int32) for s in range(1, k))
    tok = jnp.where(valid, a_id - slot * t, 0)
    dest = jnp.where(valid, a_id, na + lane)
    dest = jnp.concatenate([na + lane, dest], axis=0)
    n_used = (pend[n_experts - 1] // tm).reshape(1)
    return tok.reshape(nt, 1, tm), dest.reshape(nt + 1, 1, tm), tile_e, n_used


def _expert_kernel(te_ref, nu_ref, tokc_ref, tokn_ref, dstp_ref, dstc_ref, h_hbm,
                   wgu_f32, bgu_ref, wd_f32, bd_ref, y_hbm, xbuf, xbs, ybuf, ystage, wgu_ref, wd_ref,
                   gsem, ssem, fsem, *, s_tiles):
    j = pl.program_id(0)
    n_used = nu_ref[0]
    tm = xbs.shape[0]
    d_ff = wd_ref.shape[0]
    cast_rows = 64
    n_chunks = d_ff // FF_CHUNK
    assert n_chunks % 2 == 0
    per_chunk = tm // (n_chunks // 2)

    def slab(ref, r):
        if isinstance(r, int):
            return ref.at[pl.ds(r * s_tiles, s_tiles)]
        return ref.at[pl.ds(pl.multiple_of(r * s_tiles, s_tiles), s_tiles)]

    def gather_row(tok_ref, r, priority):
        pltpu.make_async_copy(slab(h_hbm, tok_ref[0, r]), slab(xbuf, r), gsem.at[0]).start(priority=priority)

    def scatter_row(dst_ref, src_buf, r, priority):
        pltpu.make_async_copy(slab(src_buf, r), slab(y_hbm, dst_ref[0, r]), ssem.at[0]).start(priority=priority)

    def all_rows(row_fn):
        def body(i, carry):
            row_fn(2 * i, 0)
            row_fn(2 * i + 1, 1)
            return carry
        lax.fori_loop(0, tm // 2, body, 0, unroll=4)

    def wait_rows(buf, sem):
        pltpu.make_async_copy(buf, buf, sem.at[0]).wait()

    @pl.when(j < n_used)
    def _():
        @pl.when(j == 0)
        def _():
            all_rows(lambda r, p: gather_row(tokc_ref, r, p))
            ybuf[...] = jnp.zeros(ybuf.shape, F32)

        wait_rows(xbuf, gsem)
        xbs[...] = _load_tile_major(xbuf, tm, s_tiles).astype(BF16)

        @pl.when(j >= 1)
        def _():
            wait_rows(ystage, ssem)

        ystage[...] = ybuf[...]

        @pl.when((j == 0) | (te_ref[j] != te_ref[jnp.maximum(j - 1, 0)]))
        def _():
            def cast(src, dst):
                def body(i, carry):
                    rows = pl.ds(pl.multiple_of(i * cast_rows, cast_rows), cast_rows)
                    dst[rows, :] = src[rows, :].astype(BF16)
                    return carry
                lax.fori_loop(0, src.shape[0] // cast_rows, body, 0)
            cast(wgu_f32, wgu_ref)
            cast(wd_f32, wd_ref)

        acc = None
        for c in range(n_chunks):
            gs = slice(c * FF_CHUNK, (c + 1) * FF_CHUNK)
            us = slice(d_ff + c * FF_CHUNK, d_ff + (c + 1) * FF_CHUNK)
            half = n_chunks // 2
            rows = range((c % half) * per_chunk, (c % half + 1) * per_chunk)
            for r in rows:
                if c < half:
                    gather_row(tokn_ref, r, r % 2)
                else:
                    scatter_row(dstp_ref, ystage, r, r % 2)
            gate = jnp.dot(xbs[...], wgu_ref[:, gs], preferred_element_type=F32) + bgu_ref[:, gs]
            up = jnp.dot(xbs[...], wgu_ref[:, us], preferred_element_type=F32) + bgu_ref[:, us]
            gate = jnp.minimum(gate, SWIGLU_LIMIT)
            up = jnp.clip(up, -SWIGLU_LIMIT, SWIGLU_LIMIT)
            act = (up + 1.0) * gate * jax.nn.sigmoid(SWIGLU_ALPHA * gate)
            part = jnp.dot(act.astype(BF16), wd_ref[gs, :], preferred_element_type=F32)
            acc = part if acc is None else acc + part
            pl.semaphore_signal(fsem.at[0], 1)
            pl.semaphore_wait(fsem.at[0], 1)
        _store_tile_major(ybuf, acc + bd_ref[...], s_tiles)

        @pl.when(j == n_used - 1)
        def _():
            wait_rows(ystage, ssem)
            all_rows(lambda r, p: scatter_row(dstc_ref, ybuf, r, p))
            wait_rows(ybuf, ssem)
            wait_rows(xbuf, gsem)


def _expert_call(h, groups, w_gate_up, bgu, w_down, bd, layer, tm):
    tok, dest, tile_e, n_used = groups
    depth, n_experts, d, two_f = w_gate_up.shape
    s_tiles = d // LANES
    t = h.shape[0] // s_tiles
    d_ff = two_f // 2
    nt = tok.shape[0]
    na = TOP_K * t
    assert tm % (2 * (d_ff // FF_CHUNK)) == 0
    smem_row = lambda fn: pl.BlockSpec((None, 1, tm), fn, memory_space=pltpu.SMEM)
    return pl.pallas_call(
        functools.partial(_expert_kernel, s_tiles=s_tiles),
        grid_spec=pltpu.PrefetchScalarGridSpec(
            num_scalar_prefetch=2, grid=(nt,),
            in_specs=[
                smem_row(lambda j, te, nu: (j, 0, 0)),
                smem_row(lambda j, te, nu: (jnp.minimum(j + 1, nt - 1), 0, 0)),
                smem_row(lambda j, te, nu: (j, 0, 0)),
                smem_row(lambda j, te, nu: (j + 1, 0, 0)),
                pl.BlockSpec(memory_space=pl.ANY),
                pl.BlockSpec((None, None, d, two_f), lambda j, te, nu: (layer, te[j], 0, 0)),
                pl.BlockSpec((None, None, 1, two_f), lambda j, te, nu: (layer, te[j], 0, 0)),
                pl.BlockSpec((None, None, d_ff, d), lambda j, te, nu: (layer, te[j], 0, 0)),
                pl.BlockSpec((None, None, 1, d), lambda j, te, nu: (layer, te[j], 0, 0)),
            ],
            out_specs=pl.BlockSpec(memory_space=pl.ANY),
            scratch_shapes=[pltpu.VMEM((tm * s_tiles, LANES), F32), pltpu.VMEM((tm, d), BF16),
                            pltpu.VMEM((tm * s_tiles, LANES), F32), pltpu.VMEM((tm * s_tiles, LANES), F32),
                            pltpu.VMEM((d, two_f), BF16), pltpu.VMEM((d_ff, d), BF16),
                            pltpu.SemaphoreType.DMA((1,)), pltpu.SemaphoreType.DMA((1,)),
                            pltpu.SemaphoreType.REGULAR((1,))]),
        out_shape=jax.ShapeDtypeStruct(((na + tm) * s_tiles, LANES), F32),
        compiler_params=_cparams(1),
        name="expert_ffn",
    )(tile_e, n_used, tok, tok, dest, dest, h, w_gate_up, bgu.reshape(depth, n_experts, 1, two_f), w_down,
      bd.reshape(depth, n_experts, 1, d))


def _final_kernel(seq_ref, x_ref, y0, y1, y2, y3, wcol_ref, gt_ref, g_ref, o_ref):
    del seq_ref
    x = x_ref[...] + gt_ref[...] * _weighted_expert_sum((y0, y1, y2, y3), wcol_ref, x_ref.shape)
    o_ref[...] = _rms(x) * g_ref[...]


def _final_call(blk_seq, x, y4, wcol, mod_prev, g_final, nsp, blk0, nblk):
    t, d = x.shape
    tb = TOKEN_BLOCK
    nb = t // tb
    s_tiles = d // LANES
    in_specs = [pl.BlockSpec((tb, d), lambda i, seq: (blk0 + i, 0))]
    in_specs += [pl.BlockSpec((tb * s_tiles, LANES), lambda i, seq, k=k: (k * nb + blk0 + i, 0))
                 for k in range(TOP_K)]
    in_specs += [pl.BlockSpec((tb, LANES), lambda i, seq: (blk0 + i, 0)),
                 pl.BlockSpec((None, 1, d), lambda i, seq: (5 * nsp + seq[blk0 + i], 0, 0)),
                 pl.BlockSpec((1, d), lambda i, seq: (0, 0))]
    return pl.pallas_call(
        _final_kernel,
        grid_spec=pltpu.PrefetchScalarGridSpec(
            num_scalar_prefetch=1, grid=(nblk,), in_specs=in_specs,
            out_specs=pl.BlockSpec((tb, d), lambda i, seq: (i, 0))),
        out_shape=jax.ShapeDtypeStruct((nblk * tb, d), F32),
        compiler_params=_cparams(1),
        name="final_norm",
    )(blk_seq, x, y4, y4, y4, y4, wcol, mod_prev, g_final.reshape(1, d))


def _block_tables(seq_lens, tb):
    seq, first, last = [], [], []
    for s, n in enumerate(seq_lens):
        assert n % tb == 0
        nblk = n // tb
        seq += [s] * nblk
        first += [1] + [0] * (nblk - 1)
        last += [0] * (nblk - 1) + [1]
    as_i32 = lambda z: jnp.asarray(np.asarray(z, np.int32))
    return as_i32(seq), as_i32(first), as_i32(last)


def kernel(x_prompt, x_sample, c_prompt, c_sample, w_ada, b_ada, g_norm1, w_in, conv_w, conv_b, conv_ln_g,
           conv_ln_b, g_attn_out, g_conv_out, w_out, g_norm2, w_router, b_router, w_gate_up, b_gate_up,
           w_down, b_down, g_final):
    bp, sp, d = x_prompt.shape
    bs, ss, _ = x_sample.shape
    depth = w_ada.shape[0]
    conv_ch = conv_w.shape[2]
    attn_w = (w_in.shape[2] - 2 * conv_ch) // 3
    n_experts = w_router.shape[2]
    tp, ts = bp * sp, bs * ss
    t = tp + ts
    seq_lens = [sp] * bp + [ss] * bs
    nseq = len(seq_lens)
    nsp = -(-nseq // 8) * 8
    seq_bounds = []
    start = 0
    for n in seq_lens:
        seq_bounds.append((start, start + n))
        start += n

    x = jnp.concatenate([x_prompt.reshape(tp, d), x_sample.reshape(ts, d)], axis=0)
    c_all = jnp.concatenate([c_prompt, c_sample, jnp.zeros((nsp - nseq, d), F32)], axis=0)
    mod = _ada_call(c_all, w_ada, b_ada)
    mod = mod.reshape(depth, nsp, 6, d).transpose(0, 2, 1, 3).reshape(depth, 6 * nsp, 1, d)

    seq_tok, _, _ = _block_tables(seq_lens, TOKEN_BLOCK)
    seq_mix, first_mix, last_mix = _block_tables(seq_lens, MIX_BLOCK)

    ep = -(-n_experts // LANES) * LANES
    w_router_pad = jnp.pad(w_router, ((0, 0), (0, 0), (0, ep - n_experts)))
    b_router_pad = jnp.pad(b_router, ((0, 0), (0, ep - n_experts))).reshape(depth, 1, ep)
    w_in_bf = w_in.astype(BF16)
    w_out_bf = w_out.astype(BF16)

    y4 = wcol = None
    for l in range(depth):
        x, qkv, u = _inproj_call(seq_tok, x, y4, wcol, mod[l], mod[l - 1] if l else None, g_norm1[l],
                                 w_in_bf, l, attn_w, conv_ch, nsp)
        branches = [_attn_call(q, k, v, seq_bounds) for q, k, v in qkv]
        lw = dict(conv_w=conv_w[l], conv_b=conv_b[l], conv_ln_g=conv_ln_g[l], conv_ln_b=conv_ln_b[l],
                  g_attn_out=g_attn_out[l], g_conv_out=g_conv_out[l], w_out_bf=w_out_bf, layer=l)
        x = _mix_call(seq_mix, first_mix, last_mix, x, u, branches, mod[l], lw, nsp)
        h, idx, wcol = _router_call(seq_tok, x, mod[l], g_norm2[l], w_router_pad, b_router_pad, l, n_experts, nsp)
        groups = _group_by_expert(idx, n_experts, EXPERT_TILE)
        y4 = _expert_call(h, groups, w_gate_up, b_gate_up, w_down, b_down, l, EXPERT_TILE)
    nbp = tp // TOKEN_BLOCK
    out_p = _final_call(seq_tok, x, y4, wcol, mod[depth - 1], g_final, nsp, 0, nbp)
    out_s = _final_call(seq_tok, x, y4, wcol, mod[depth - 1], g_final, nsp, nbp, ts // TOKEN_BLOCK)
    return out_p.reshape(bp, sp, d), out_s.reshape(bs, ss, d)
```

```python
import functools

import numpy as np
import jax
import jax.numpy as jnp
from jax import lax
from jax.experimental import pallas as pl
from jax.experimental.pallas import tpu as pltpu

F32 = jnp.float32
BF16 = jnp.bfloat16
HIGHEST = lax.Precision.HIGHEST

LANES = 128
HEAD_DIM = 64
HEAD_PAIR = 2 * HEAD_DIM
CONV_WIDTH = 31
CONV_HALO = 16
DILATED_BRANCHES = ((128, 1), (512, 4), (2048, 16))
HALF_WINDOW = 64
TOP_K = 4
SWIGLU_ALPHA = 1.702
SWIGLU_LIMIT = 7.0
NORM_EPS = 1e-6
NEG_BIG = -1e30

TOKEN_BLOCK = 512
MIX_BLOCK = 512
ATTN_BLOCK = 512
ATTN_SUB = 128
EXPERT_TILE = 512
FF_CHUNK = 512
VMEM_LIMIT = 56 * 1024 * 1024


def _cparams(n_axes):
    return pltpu.CompilerParams(dimension_semantics=("arbitrary",) * n_axes, vmem_limit_bytes=VMEM_LIMIT)


def _rms(x):
    return x * lax.rsqrt(jnp.mean(x * x, axis=-1, keepdims=True) + NORM_EPS)


def _load_tile_major(ref, n_rows, s_tiles):
    return jnp.concatenate([ref[pl.ds(s, n_rows, stride=s_tiles), :] for s in range(s_tiles)], axis=1)


def _store_tile_major(ref, val, s_tiles):
    n_rows = val.shape[0]
    for s in range(s_tiles):
        ref[pl.ds(s, n_rows, stride=s_tiles), :] = val[:, LANES * s:LANES * (s + 1)]


def _weighted_expert_sum(y_refs, wcol_ref, shape):
    tb, d = shape
    w = wcol_ref[...]
    terms = [w[:, k:k + 1] * _load_tile_major(y, tb, d // LANES) for k, y in enumerate(y_refs)]
    return (terms[0] + terms[1]) + (terms[2] + terms[3])


def _ada_kernel(c_ref, w_ref, b_ref, o_ref):
    c = c_ref[...]
    s = c * jax.nn.sigmoid(c)
    o_ref[...] = jnp.dot(s, w_ref[...], precision=HIGHEST, preferred_element_type=F32) + b_ref[...]


def _ada_call(c_pad, w_ada, b_ada):
    depth, d, n = w_ada.shape
    nsp = c_pad.shape[0]
    tn = d
    return pl.pallas_call(
        _ada_kernel,
        grid=(depth, n // tn),
        in_specs=[
            pl.BlockSpec((nsp, d), lambda l, j: (0, 0)),
            pl.BlockSpec((None, d, tn), lambda l, j: (l, 0, j)),
            pl.BlockSpec((None, 1, tn), lambda l, j: (l, 0, j)),
        ],
        out_specs=pl.BlockSpec((None, nsp, tn), lambda l, j: (l, 0, j)),
        out_shape=jax.ShapeDtypeStruct((depth, nsp, n), F32),
        compiler_params=_cparams(2),
        name="ada_mod",
    )(c_pad, w_ada, b_ada.reshape(depth, 1, n))


def _inproj_kernel(seq_ref, *refs, attn_w, conv_ch, combine, dilations, n_first):
    del seq_ref
    n_reg = 3 * (len(dilations) - 1)
    if combine:
        x_ref, y0, y1, y2, y3, wcol_ref, gt_ref, g_ref, sc_ref, sh_ref, w_ref, xo_ref = refs[:12]
        rest = refs[12:]
        tb = x_ref.shape[0]
        x = x_ref[...] + gt_ref[...] * _weighted_expert_sum((y0, y1, y2, y3), wcol_ref, x_ref.shape)
        xo_ref[...] = x
    else:
        xa_ref, xb_ref, g_ref, sc_ref, sh_ref, w_ref, xo_ref = refs[:7]
        rest = refs[7:]
        tb = xa_ref.shape[0]
        x = jnp.where(pl.program_id(0) < n_first, xa_ref[...], xb_ref[...])
        xo_ref[...] = x
    q_ref, k_ref, v_ref = rest[:3]
    regrouped = rest[3:3 + n_reg]
    u_ref, pbuf = rest[3 + n_reg:]
    h = _rms(x) * g_ref[...]
    h = h * (1.0 + sc_ref[...]) + sh_ref[...]
    proj = jnp.dot(h.astype(BF16), w_ref[...], preferred_element_type=F32)
    a = attn_w
    q = proj[:, :a] * (HEAD_DIM ** -0.5)
    q_ref[0] = q.astype(BF16)
    k_ref[0] = proj[:, a:2 * a].astype(BF16)
    v_ref[0] = proj[:, 2 * a:3 * a].astype(BF16)
    cv = proj[:, 3 * a:3 * a + conv_ch]
    cg = proj[:, 3 * a + conv_ch:]
    u_ref[...] = (cv * jax.nn.sigmoid(cg)).astype(BF16)
    tiles_per = a // LANES
    for c in range(3 * tiles_per):
        cols = slice(LANES * c, LANES * (c + 1))
        pbuf[c] = q[:, cols] if c < tiles_per else proj[:, cols]
    for bi, dil in enumerate(dilations[1:]):
        for r in range(dil):
            for c in range(3 * tiles_per):
                dst = regrouped[3 * bi + c // tiles_per]
                cols = slice(LANES * (c % tiles_per), LANES * (c % tiles_per + 1))
                dst[r, :, cols] = pbuf[c, pl.ds(r, tb // dil, stride=dil), :].astype(BF16)


def _mod_spec(comp, nsp, d):
    return pl.BlockSpec((None, 1, d), lambda i, seq: (comp * nsp + seq[i], 0, 0))


def _inproj_call(blk_seq, x, y4, wcol, mod_l, mod_prev, g_norm, w_in_bf, layer, attn_w, conv_ch, nsp):
    tb = TOKEN_BLOCK
    combine = y4 is not None
    if combine:
        t, d = x.shape
        n_first = None
    else:
        xa, xb = x
        d = xa.shape[1]
        t = xa.shape[0] + xb.shape[0]
        n_first = xa.shape[0] // tb
    nb = t // tb
    s_tiles = d // LANES
    dilations = tuple(dil for _, dil in DILATED_BRANCHES)
    assert dilations[0] == 1
    row = pl.BlockSpec((tb, d), lambda i, seq: (i, 0))
    if combine:
        in_specs = [row]
        args = [x]
    else:
        in_specs = [pl.BlockSpec((tb, d), lambda i, seq: (jnp.minimum(i, n_first - 1), 0)),
                    pl.BlockSpec((tb, d), lambda i, seq: (jnp.maximum(i - n_first, 0), 0))]
        args = [xa, xb]
    if combine:
        for k in range(TOP_K):
            in_specs.append(pl.BlockSpec((tb * s_tiles, LANES), lambda i, seq, k=k: (k * nb + i, 0)))
            args.append(y4)
        in_specs += [pl.BlockSpec((tb, LANES), lambda i, seq: (i, 0)), _mod_spec(5, nsp, d)]
        args += [wcol, mod_prev]
    in_specs += [
        pl.BlockSpec((1, d), lambda i, seq: (0, 0)),
        _mod_spec(1, nsp, d),
        _mod_spec(0, nsp, d),
        pl.BlockSpec((None,) + w_in_bf.shape[1:], lambda i, seq: (layer, 0, 0)),
    ]
    args += [g_norm.reshape(1, d), mod_l, mod_l, w_in_bf]
    out_shapes = [jax.ShapeDtypeStruct((t, d), F32)]
    out_specs = [row]
    for dil in dilations:
        for _ in range(3):
            out_shapes.append(jax.ShapeDtypeStruct((dil, t // dil, attn_w), BF16))
            out_specs.append(pl.BlockSpec((dil, tb // dil, attn_w), lambda i, seq: (0, i, 0)))
    out_shapes.append(jax.ShapeDtypeStruct((t, conv_ch), BF16))
    out_specs.append(pl.BlockSpec((tb, conv_ch), lambda i, seq: (i, 0)))
    outs = pl.pallas_call(
        functools.partial(_inproj_kernel, attn_w=attn_w, conv_ch=conv_ch, combine=combine, dilations=dilations,
                          n_first=n_first),
        grid_spec=pltpu.PrefetchScalarGridSpec(
            num_scalar_prefetch=1, grid=(nb,), in_specs=in_specs, out_specs=out_specs,
            scratch_shapes=[pltpu.VMEM((3 * attn_w // LANES, tb, LANES), F32)]),
        out_shape=out_shapes,
        compiler_params=_cparams(1),
        name="in_proj",
    )(blk_seq, *args)
    x, outs = outs[0], outs[1:]
    qkv = [tuple(outs[3 * b:3 * b + 3]) for b in range(len(dilations))]
    return x, qkv, outs[-1]


def _attn_kernel(lo_ref, hi_ref, q_ref, kp_ref, kc_ref, kn_ref, vp_ref, vc_ref, vn_ref, bias_ref,
                 o_ref, lse_ref, *, n_pairs):
    i = pl.program_id(1)
    lb = q_ref.shape[0]
    sb = bias_ref.shape[1]
    kw = sb + 2 * HALF_WINDOW
    lane = lax.broadcasted_iota(jnp.int32, (1, HEAD_PAIR), 1)
    for j in range(n_pairs):
        sl = slice(HEAD_PAIR * j, HEAD_PAIR * (j + 1))
        kfull = jnp.concatenate([kp_ref[:, sl], kc_ref[:, sl], kn_ref[:, sl]], axis=0)
        vfull = jnp.concatenate([vp_ref[:, sl], vc_ref[:, sl], vn_ref[:, sl]], axis=0)
        for t in range(lb // sb):
            rows = slice(t * sb, (t + 1) * sb)
            kpos = i * lb + t * sb - HALF_WINDOW + lax.broadcasted_iota(jnp.int32, (1, kw), 1)
            col_ok = (kpos >= lo_ref[i]) & (kpos < hi_ref[i])
            qp = q_ref[rows, sl]
            kwin = kfull[t * sb:t * sb + kw]
            vwin = vfull[t * sb:t * sb + kw]
            o_pair = None
            lse_pair = None
            for b in range(2):
                in_head = (lane >= HEAD_DIM * b) & (lane < HEAD_DIM * (b + 1))
                head_mask = jnp.where(in_head, 1.0, 0.0).astype(BF16)
                s = lax.dot_general(qp * head_mask, kwin, (((1,), (1,)), ((), ())),
                                    preferred_element_type=F32)
                s = s + bias_ref[2 * j + b]
                s = jnp.where(col_ok, s, NEG_BIG)
                m = jnp.max(s, axis=-1, keepdims=True)
                p = jnp.exp(s - m)
                den = jnp.sum(p, axis=-1, keepdims=True)
                pv = jnp.dot(p.astype(BF16), vwin, preferred_element_type=F32)
                o_b = pv * (1.0 / den)
                lse_b = jnp.broadcast_to(m + jnp.log(den), (sb, HEAD_PAIR))
                if b == 0:
                    o_pair, lse_pair = o_b, lse_b
                else:
                    o_pair = jnp.where(in_head, o_b, o_pair)
                    lse_pair = jnp.where(in_head, lse_b, lse_pair)
            o_ref[rows, sl] = o_pair.astype(BF16)
            lse_ref[rows, sl] = lse_pair


def _attn_bias(dilation, n_heads, lb):
    kw = lb + 2 * HALF_WINDOW
    rel = (np.arange(kw)[None, :] - HALF_WINDOW) - np.arange(lb)[:, None]
    slopes = 2.0 ** (-8.0 * np.arange(1, n_heads + 1, dtype=np.float64) / n_heads)
    bias = -slopes[:, None, None] * (dilation * np.abs(rel))[None].astype(np.float64)
    bias = np.where((np.abs(rel) <= HALF_WINDOW)[None], bias, NEG_BIG)
    return jnp.asarray(bias.astype(np.float32))


def _attn_call(q, k, v, seq_bounds):
    dilation, rows, a = q.shape
    lb = ATTN_BLOCK
    n_heads = a // HEAD_DIM
    nb = rows // lb
    sub = lb // HALF_WINDOW
    n_halo_blocks = rows // HALF_WINDOW
    lo = np.zeros((nb,), np.int32)
    hi = np.zeros((nb,), np.int32)
    for (t_lo, t_hi) in seq_bounds:
        assert t_lo % (dilation * lb) == 0 and t_hi % (dilation * lb) == 0
        lo[t_lo // dilation // lb:t_hi // dilation // lb] = t_lo // dilation
        hi[t_lo // dilation // lb:t_hi // dilation // lb] = t_hi // dilation
    cur = pl.BlockSpec((None, lb, a), lambda r, i, lo, hi: (r, i, 0))
    prev = pl.BlockSpec((None, HALF_WINDOW, a), lambda r, i, lo, hi: (r, jnp.maximum(i * sub - 1, 0), 0))
    nxt = pl.BlockSpec((None, HALF_WINDOW, a),
                       lambda r, i, lo, hi: (r, jnp.minimum((i + 1) * sub, n_halo_blocks - 1), 0))
    bias = _attn_bias(dilation, n_heads, min(ATTN_SUB, lb))
    return pl.pallas_call(
        functools.partial(_attn_kernel, n_pairs=a // HEAD_PAIR),
        grid_spec=pltpu.PrefetchScalarGridSpec(
            num_scalar_prefetch=2, grid=(dilation, nb),
            in_specs=[cur, prev, cur, nxt, prev, cur, nxt,
                      pl.BlockSpec(bias.shape, lambda r, i, lo, hi: (0, 0, 0))],
            out_specs=[cur, cur]),
        out_shape=[jax.ShapeDtypeStruct((dilation, rows, a), BF16),
                   jax.ShapeDtypeStruct((dilation, rows, a), F32)],
        compiler_params=_cparams(2),
        name=f"attn_d{dilation}",
    )(jnp.asarray(lo), jnp.asarray(hi), q, k, k, k, v, v, v, bias)


def _mix_kernel(seq_ref, first_ref, last_ref, x_ref, up_ref, uc_ref, un_ref, *refs, attn_w, dilations):
    del seq_ref
    nbr = len(dilations)
    branch_refs = refs[:2 * nbr]
    cw_ref, cb_ref, lng_ref, lnb_ref, ga_ref, gc_ref, wo_ref, gt_ref, xo_ref = refs[2 * nbr:2 * nbr + 9]
    bufs = refs[2 * nbr + 9:]
    i = pl.program_id(0)
    tb = x_ref.shape[0]

    def natural(ref, dil, buf):
        if dil == 1:
            return ref[0].astype(F32)
        n_tiles = buf.shape[0]
        for r in range(dil):
            plane = ref[r].astype(F32)
            for c in range(n_tiles):
                buf[c, pl.ds(r, tb // dil, stride=dil), :] = plane[:, LANES * c:LANES * (c + 1)]
        return jnp.concatenate([buf[c] for c in range(n_tiles)], axis=1)

    outs, lses = [], []
    for b, dil in enumerate(dilations):
        outs.append(natural(branch_refs[2 * b], dil, bufs[2 * b]))
        lses.append(natural(branch_refs[2 * b + 1], dil, bufs[2 * b + 1]))
    m = functools.reduce(jnp.maximum, lses)
    es = [jnp.exp(l - m) for l in lses]
    num = functools.reduce(lambda p, q: p + q, [e * o for e, o in zip(es, outs)])
    attn = num * (1.0 / functools.reduce(lambda p, q: p + q, es))
    attn_n = _rms(attn) * ga_ref[...]

    keep_prev = jnp.where(first_ref[i] == 1, 0.0, 1.0)
    keep_next = jnp.where(last_ref[i] == 1, 0.0, 1.0)
    win = jnp.concatenate([up_ref[...].astype(F32) * keep_prev, uc_ref[...].astype(F32),
                           un_ref[...].astype(F32) * keep_next], axis=0)
    n = tb + 2 * CONV_HALO
    acc = None
    for rho in range(8):
        shifted = win if rho == 0 else pltpu.roll(win, n - rho, axis=0)
        for blk in range(2 * CONV_HALO // 8):
            tap = 8 * blk + rho - (CONV_HALO - CONV_WIDTH // 2)
            if 0 <= tap < CONV_WIDTH:
                term = cw_ref[tap:tap + 1, :] * shifted[8 * blk:8 * blk + tb]
                acc = term if acc is None else acc + term
    conv = acc + cb_ref[...]
    mu = jnp.mean(conv, axis=-1, keepdims=True)
    xc = conv - mu
    y = xc * lax.rsqrt(jnp.mean(xc * xc, axis=-1, keepdims=True) + NORM_EPS) * lng_ref[...] + lnb_ref[...]
    y = y * jax.nn.sigmoid(y)
    conv_n = _rms(y) * gc_ref[...]

    out = jnp.dot(attn_n.astype(BF16), wo_ref[:attn_w, :], preferred_element_type=F32)
    out = out + jnp.dot(conv_n.astype(BF16), wo_ref[attn_w:, :], preferred_element_type=F32)
    xo_ref[...] = x_ref[...] + gt_ref[...] * out


def _mix_call(blk_seq, first, last, x, u, branches, mod_l, lw, nsp):
    t, d = x.shape
    c = u.shape[1]
    a = branches[0][0].shape[2]
    tb = MIX_BLOCK
    nb = t // tb
    sub = tb // CONV_HALO
    n_halo_blocks = t // CONV_HALO
    dilations = tuple(o.shape[0] for o, _ in branches)
    row = lambda w: pl.BlockSpec((tb, w), lambda i, *_: (i, 0))
    vec = lambda w: pl.BlockSpec((1, w), lambda i, *_: (0, 0))
    in_specs = [
        row(d),
        pl.BlockSpec((CONV_HALO, c), lambda i, *_: (jnp.maximum(i * sub - 1, 0), 0)),
        row(c),
        pl.BlockSpec((CONV_HALO, c), lambda i, *_: (jnp.minimum((i + 1) * sub, n_halo_blocks - 1), 0)),
    ]
    args = [x, u, u, u]
    for (o, lse), dil in zip(branches, dilations):
        plane = pl.BlockSpec((dil, tb // dil, a), lambda i, *_: (0, i, 0))
        in_specs += [plane, plane]
        args += [o, lse]
    in_specs += [
        pl.BlockSpec((CONV_WIDTH, c), lambda i, *_: (0, 0)),
        vec(c), vec(c), vec(c), vec(a), vec(c),
        pl.BlockSpec((None, a + c, d), lambda i, *_: (lw["layer"], 0, 0)),
        pl.BlockSpec((None, 1, d), lambda i, seq, *_: (2 * nsp + seq[i], 0, 0)),
    ]
    args += [lw["conv_w"], lw["conv_b"].reshape(1, c), lw["conv_ln_g"].reshape(1, c),
             lw["conv_ln_b"].reshape(1, c), lw["g_attn_out"].reshape(1, a), lw["g_conv_out"].reshape(1, c),
             lw["w_out_bf"], mod_l]
    return pl.pallas_call(
        functools.partial(_mix_kernel, attn_w=a, dilations=dilations),
        grid_spec=pltpu.PrefetchScalarGridSpec(
            num_scalar_prefetch=3, grid=(nb,), in_specs=in_specs, out_specs=row(d),
            scratch_shapes=[pltpu.VMEM((a // LANES, tb, LANES), F32) for _ in range(2 * len(dilations))]),
        out_shape=jax.ShapeDtypeStruct((t, d), F32),
        compiler_params=_cparams(1),
        name="mix_out",
    )(blk_seq, first, last, *args)


def _router_kernel(seq_ref, x_ref, g_ref, sc_ref, sh_ref, wr_ref, br_ref, h_ref, idx_ref, wcol_ref, *, n_experts):
    del seq_ref
    tb, d = x_ref.shape
    h = _rms(x_ref[...]) * g_ref[...]
    h = h * (1.0 + sc_ref[...]) + sh_ref[...]
    _store_tile_major(h_ref, h, d // LANES)
    w = wr_ref[...]
    h_hi, w_hi = h.astype(BF16), w.astype(BF16)
    h_lo = (h - h_hi.astype(F32)).astype(BF16)
    w_lo = (w - w_hi.astype(F32)).astype(BF16)
    dot = functools.partial(jnp.dot, preferred_element_type=F32)
    logits = dot(h_hi, w_hi) + (dot(h_lo, w_hi) + dot(h_hi, w_lo)) + br_ref[...]
    lt = logits.T[:n_experts]
    eio = lax.broadcasted_iota(jnp.int32, lt.shape, 0)
    vals, idxs = [], []
    for _ in range(TOP_K):
        m = jnp.max(lt, axis=0, keepdims=True)
        ix = jnp.min(jnp.where(lt == m, eio, n_experts), axis=0, keepdims=True)
        vals.append(m)
        idxs.append(ix)
        lt = jnp.where(eio == ix, -jnp.inf, lt)
    es = [jnp.exp(v - vals[0]) for v in vals]
    inv = 1.0 / (es[0] + es[1] + es[2] + es[3])
    for k in range(TOP_K):
        idx_ref[k:k + 1, :] = idxs[k]
    sub = lax.broadcasted_iota(jnp.int32, (8, tb), 0)
    wrows = jnp.zeros((8, tb), F32)
    for k in range(TOP_K):
        wrows = jnp.where(sub == k, es[k] * inv, wrows)
    wcol_ref[...] = jnp.concatenate([wrows, jnp.zeros((LANES - 8, tb), F32)], axis=0).T


def _router_call(blk_seq, x, mod_l, g_norm, w_router_pad, b_router_pad, layer, n_experts, nsp):
    t, d = x.shape
    tb = TOKEN_BLOCK
    s_tiles = d // LANES
    ep = w_router_pad.shape[2]
    row = pl.BlockSpec((tb, d), lambda i, seq: (i, 0))
    return pl.pallas_call(
        functools.partial(_router_kernel, n_experts=n_experts),
        grid_spec=pltpu.PrefetchScalarGridSpec(
            num_scalar_prefetch=1, grid=(t // tb,),
            in_specs=[row, pl.BlockSpec((1, d), lambda i, seq: (0, 0)), _mod_spec(4, nsp, d), _mod_spec(3, nsp, d),
                      pl.BlockSpec((None, d, ep), lambda i, seq: (layer, 0, 0)),
                      pl.BlockSpec((None, 1, ep), lambda i, seq: (layer, 0, 0))],
            out_specs=[pl.BlockSpec((tb * s_tiles, LANES), lambda i, seq: (i, 0)),
                       pl.BlockSpec((TOP_K, tb), lambda i, seq: (0, i)),
                       pl.BlockSpec((tb, LANES), lambda i, seq: (i, 0))]),
        out_shape=[jax.ShapeDtypeStruct((t * s_tiles, LANES), F32), jax.ShapeDtypeStruct((TOP_K, t), jnp.int32),
                   jax.ShapeDtypeStruct((t, LANES), F32)],
        compiler_params=_cparams(1),
        name="router",
    )(blk_seq, x, g_norm.reshape(1, d), mod_l, mod_l, w_router_pad, b_router_pad)


def _group_by_expert(idx, n_experts, tm):
    k, t = idx.shape
    na = k * t
    nt = na // tm + n_experts
    a_bits = int(np.ceil(np.log2(na)))
    assert n_experts << a_bits < 2 ** 31
    e_flat = idx.reshape(na)
    order = jnp.sort((e_flat << a_bits) | jnp.arange(na, dtype=jnp.int32)) & ((1 << a_bits) - 1)
    experts = jnp.arange(n_experts, dtype=jnp.int32)
    counts = jnp.sum((e_flat[None, :] == experts[:, None]).astype(jnp.int32), axis=1)
    gstart = jnp.cumsum(counts) - counts
    pcounts = ((counts + tm - 1) // tm) * tm
    pend = jnp.cumsum(pcounts)
    pstart = pend - pcounts
    tile_start = jnp.arange(nt, dtype=jnp.int32) * tm
    tile_e = jnp.minimum(jnp.sum((tile_start[:, None] >= pend[None, :]).astype(jnp.int32), axis=1), n_experts - 1)
    rank0 = tile_start - pstart[tile_e]
    n_valid = counts[tile_e] - rank0
    src0 = gstart[tile_e] + rank0
    lane = jnp.arange(tm, dtype=jnp.int32)[None, :]
    valid = lane < n_valid[:, None]
    a_id = order[jnp.clip(src0[:, None] + lane, 0, na - 1)]
    slot = sum((a_id >= s * t).astype(jnp.int32) for s in range(1, k))
    tok = jnp.where(valid, a_id - slot * t, 0)
    dest = jnp.where(valid, a_id, na + lane)
    dest = jnp.concatenate([na + lane, dest], axis=0)
    n_used = (pend[n_experts - 1] // tm).reshape(1)
    return tok.reshape(nt, 1, tm), dest.reshape(nt + 1, 1, tm), tile_e, n_used


def _expert_kernel(te_ref, nu_ref, tokc_ref, tokn_ref, dstp_ref, dstc_ref, h_hbm,
                   wgu_f32, bgu_ref, wd_f32, bd_ref, y_hbm, xbuf, xbs, ybuf, ystage, wgu_ref, wd_ref,
                   gsem, ssem, fsem, *, s_tiles):
    j = pl.program_id(0)
    n_used = nu_ref[0]
    tm = xbs.shape[0]
    d_ff = wd_ref.shape[0]
    cast_rows = 64
    n_chunks = d_ff // FF_CHUNK
    assert n_chunks % 2 == 0
    per_chunk = tm // (n_chunks // 2)

    def slab(ref, r):
        if isinstance(r, int):
            return ref.at[pl.ds(r * s_tiles, s_tiles)]
        return ref.at[pl.ds(pl.multiple_of(r * s_tiles, s_tiles), s_tiles)]

    def gather_row(tok_ref, r, priority):
        pltpu.make_async_copy(slab(h_hbm, tok_ref[0, r]), slab(xbuf, r), gsem.at[0]).start(priority=priority)

    def scatter_row(dst_ref, src_buf, r, priority):
        pltpu.make_async_copy(slab(src_buf, r), slab(y_hbm, dst_ref[0, r]), ssem.at[0]).start(priority=priority)

    def all_rows(row_fn):
        def body(i, carry):
            row_fn(2 * i, 0)
            row_fn(2 * i + 1, 1)
            return carry
        lax.fori_loop(0, tm // 2, body, 0, unroll=4)

    def wait_rows(buf, sem):
        pltpu.make_async_copy(buf, buf, sem.at[0]).wait()

    @pl.when(j < n_used)
    def _():
        @pl.when(j == 0)
        def _():
            all_rows(lambda r, p: gather_row(tokc_ref, r, p))
            ybuf[...] = jnp.zeros(ybuf.shape, F32)

        wait_rows(xbuf, gsem)
        xbs[...] = _load_tile_major(xbuf, tm, s_tiles).astype(BF16)

        @pl.when(j >= 1)
        def _():
            wait_rows(ystage, ssem)

        ystage[...] = ybuf[...]

        @pl.when((j == 0) | (te_ref[j] != te_ref[jnp.maximum(j - 1, 0)]))
        def _():
            def cast(src, dst):
                def body(i, carry):
                    rows = pl.ds(pl.multiple_of(i * cast_rows, cast_rows), cast_rows)
                    dst[rows, :] = src[rows, :].astype(BF16)
                    return carry
                lax.fori_loop(0, src.shape[0] // cast_rows, body, 0)
            cast(wgu_f32, wgu_ref)
            cast(wd_f32, wd_ref)

        acc = None
        for c in range(n_chunks):
            gs = slice(c * FF_CHUNK, (c + 1) * FF_CHUNK)
            us = slice(d_ff + c * FF_CHUNK, d_ff + (c + 1) * FF_CHUNK)
            half = n_chunks // 2
            rows = range((c % half) * per_chunk, (c % half + 1) * per_chunk)
            for r in rows:
                if c < half:
                    gather_row(tokn_ref, r, r % 2)
                else:
                    scatter_row(dstp_ref, ystage, r, r % 2)
            gate = jnp.dot(xbs[...], wgu_ref[:, gs], preferred_element_type=F32) + bgu_ref[:, gs]
            up = jnp.dot(xbs[...], wgu_ref[:, us], preferred_element_type=F32) + bgu_ref[:, us]
            gate = jnp.minimum(gate, SWIGLU_LIMIT)
            up = jnp.clip(up, -SWIGLU_LIMIT, SWIGLU_LIMIT)
            act = (up + 1.0) * gate * jax.nn.sigmoid(SWIGLU_ALPHA * gate)
            part = jnp.dot(act.astype(BF16), wd_ref[gs, :], preferred_element_type=F32)
            acc = part if acc is None else acc + part
            pl.semaphore_signal(fsem.at[0], 1)
            pl.semaphore_wait(fsem.at[0], 1)
        _store_tile_major(ybuf, acc + bd_ref[...], s_tiles)

        @pl.when(j == n_used - 1)
        def _():
            wait_rows(ystage, ssem)
            all_rows(lambda r, p: scatter_row(dstc_ref, ybuf, r, p))
            wait_rows(ybuf, ssem)
            wait_rows(xbuf, gsem)


def _expert_call(h, groups, w_gate_up, bgu, w_down, bd, layer, tm):
    tok, dest, tile_e, n_used = groups
    depth, n_experts, d, two_f = w_gate_up.shape
    s_tiles = d // LANES
    t = h.shape[0] // s_tiles
    d_ff = two_f // 2
    nt = tok.shape[0]
    na = TOP_K * t
    assert tm % (2 * (d_ff // FF_CHUNK)) == 0
    smem_row = lambda fn: pl.BlockSpec((None, 1, tm), fn, memory_space=pltpu.SMEM)
    return pl.pallas_call(
        functools.partial(_expert_kernel, s_tiles=s_tiles),
        grid_spec=pltpu.PrefetchScalarGridSpec(
            num_scalar_prefetch=2, grid=(nt,),
            in_specs=[
                smem_row(lambda j, te, nu: (j, 0, 0)),
                smem_row(lambda j, te, nu: (jnp.minimum(j + 1, nt - 1), 0, 0)),
                smem_row(lambda j, te, nu: (j, 0, 0)),
                smem_row(lambda j, te, nu: (j + 1, 0, 0)),
                pl.BlockSpec(memory_space=pl.ANY),
                pl.BlockSpec((None, None, d, two_f), lambda j, te, nu: (layer, te[j], 0, 0)),
                pl.BlockSpec((None, None, 1, two_f), lambda j, te, nu: (layer, te[j], 0, 0)),
                pl.BlockSpec((None, None, d_ff, d), lambda j, te, nu: (layer, te[j], 0, 0)),
                pl.BlockSpec((None, None, 1, d), lambda j, te, nu: (layer, te[j], 0, 0)),
            ],
            out_specs=pl.BlockSpec(memory_space=pl.ANY),
            scratch_shapes=[pltpu.VMEM((tm * s_tiles, LANES), F32), pltpu.VMEM((tm, d), BF16),
                            pltpu.VMEM((tm * s_tiles, LANES), F32), pltpu.VMEM((tm * s_tiles, LANES), F32),
                            pltpu.VMEM((d, two_f), BF16), pltpu.VMEM((d_ff, d), BF16),
                            pltpu.SemaphoreType.DMA((1,)), pltpu.SemaphoreType.DMA((1,)),
                            pltpu.SemaphoreType.REGULAR((1,))]),
        out_shape=jax.ShapeDtypeStruct(((na + tm) * s_tiles, LANES), F32),
        compiler_params=_cparams(1),
        name="expert_ffn",
    )(tile_e, n_used, tok, tok, dest, dest, h, w_gate_up, bgu.reshape(depth, n_experts, 1, two_f), w_down,
      bd.reshape(depth, n_experts, 1, d))


def _final_kernel(seq_ref, x_ref, y0, y1, y2, y3, wcol_ref, gt_ref, g_ref, o_ref):
    del seq_ref
    x = x_ref[...] + gt_ref[...] * _weighted_expert_sum((y0, y1, y2, y3), wcol_ref, x_ref.shape)
    o_ref[...] = _rms(x) * g_ref[...]


def _final_call(blk_seq, x, y4, wcol, mod_prev, g_final, nsp, blk0, nblk):
    t, d = x.shape
    tb = TOKEN_BLOCK
    nb = t // tb
    s_tiles = d // LANES
    in_specs = [pl.BlockSpec((tb, d), lambda i, seq: (blk0 + i, 0))]
    in_specs += [pl.BlockSpec((tb * s_tiles, LANES), lambda i, seq, k=k: (k * nb + blk0 + i, 0))
                 for k in range(TOP_K)]
    in_specs += [pl.BlockSpec((tb, LANES), lambda i, seq: (blk0 + i, 0)),
                 pl.BlockSpec((None, 1, d), lambda i, seq: (5 * nsp + seq[blk0 + i], 0, 0)),
                 pl.BlockSpec((1, d), lambda i, seq: (0, 0))]
    return pl.pallas_call(
        _final_kernel,
        grid_spec=pltpu.PrefetchScalarGridSpec(
            num_scalar_prefetch=1, grid=(nblk,), in_specs=in_specs,
            out_specs=pl.BlockSpec((tb, d), lambda i, seq: (i, 0))),
        out_shape=jax.ShapeDtypeStruct((nblk * tb, d), F32),
        compiler_params=_cparams(1),
        name="final_norm",
    )(blk_seq, x, y4, y4, y4, y4, wcol, mod_prev, g_final.reshape(1, d))


def _block_tables(seq_lens, tb):
    seq, first, last = [], [], []
    for s, n in enumerate(seq_lens):
        assert n % tb == 0
        nblk = n // tb
        seq += [s] * nblk
        first += [1] + [0] * (nblk - 1)
        last += [0] * (nblk - 1) + [1]
    as_i32 = lambda z: jnp.asarray(np.asarray(z, np.int32))
    return as_i32(seq), as_i32(first), as_i32(last)


def kernel(x_prompt, x_sample, c_prompt, c_sample, w_ada, b_ada, g_norm1, w_in, conv_w, conv_b, conv_ln_g,
           conv_ln_b, g_attn_out, g_conv_out, w_out, g_norm2, w_router, b_router, w_gate_up, b_gate_up,
           w_down, b_down, g_final):
    bp, sp, d = x_prompt.shape
    bs, ss, _ = x_sample.shape
    depth = w_ada.shape[0]
    conv_ch = conv_w.shape[2]
    attn_w = (w_in.shape[2] - 2 * conv_ch) // 3
    n_experts = w_router.shape[2]
    tp, ts = bp * sp, bs * ss
    t = tp + ts
    seq_lens = [sp] * bp + [ss] * bs
    nseq = len(seq_lens)
    nsp = -(-nseq // 8) * 8
    seq_bounds = []
    start = 0
    for n in seq_lens:
        seq_bounds.append((start, start + n))
        start += n

    x = (x_prompt.reshape(tp, d), x_sample.reshape(ts, d))
    c_all = jnp.concatenate([c_prompt, c_sample, jnp.zeros((nsp - nseq, d), F32)], axis=0)
    mod = _ada_call(c_all, w_ada, b_ada)
    mod = mod.reshape(depth, nsp, 6, d).transpose(0, 2, 1, 3).reshape(depth, 6 * nsp, 1, d)

    seq_tok, _, _ = _block_tables(seq_lens, TOKEN_BLOCK)
    seq_mix, first_mix, last_mix = _block_tables(seq_lens, MIX_BLOCK)

    ep = -(-n_experts // LANES) * LANES
    w_router_pad = jnp.pad(w_router, ((0, 0), (0, 0), (0, ep - n_experts)))
    b_router_pad = jnp.pad(b_router, ((0, 0), (0, ep - n_experts))).reshape(depth, 1, ep)
    w_in_bf = w_in.astype(BF16)
    w_out_bf = w_out.astype(BF16)

    y4 = wcol = None
    for l in range(depth):
        x, qkv, u = _inproj_call(seq_tok, x, y4, wcol, mod[l], mod[l - 1] if l else None, g_norm1[l],
                                 w_in_bf, l, attn_w, conv_ch, nsp)
        branches = [_attn_call(q, k, v, seq_bounds) for q, k, v in qkv]
        lw = dict(conv_w=conv_w[l], conv_b=conv_b[l], conv_ln_g=conv_ln_g[l], conv_ln_b=conv_ln_b[l],
                  g_attn_out=g_attn_out[l], g_conv_out=g_conv_out[l], w_out_bf=w_out_bf, layer=l)
        x = _mix_call(seq_mix, first_mix, last_mix, x, u, branches, mod[l], lw, nsp)
        h, idx, wcol = _router_call(seq_tok, x, mod[l], g_norm2[l], w_router_pad, b_router_pad, l, n_experts, nsp)
        groups = _group_by_expert(idx, n_experts, EXPERT_TILE)
        y4 = _expert_call(h, groups, w_gate_up, b_gate_up, w_down, b_down, l, EXPERT_TILE)
    nbp = tp // TOKEN_BLOCK
    out_p = _final_call(seq_tok, x, y4, wcol, mod[depth - 1], g_final, nsp, 0, nbp)
    out_s = _final_call(seq_tok, x, y4, wcol, mod[depth - 1], g_final, nsp, nbp, ts // TOKEN_BLOCK)
    return out_p.reshape(bp, sp, d), out_s.reshape(bs, ss, d)
```
